```python
import math
import jax
import jax.numpy as jnp
from jax import lax
import numpy as np

D_MODEL = 1024
BATCH = 16
SEQ = 2048
DEPTH = 2
DEC_BATCH = 32
DEC_SEQ = 8
PAST_LEN = 16384
PAGE_SIZE = 128

N_BRANCH = 4
MIX_W = D_MODEL // N_BRANCH
NSA_H = 4
HD = MIX_W // NSA_H
CMP_BLOCK = 32
SEL_BLOCK = 64
CMP_PER_SEL = SEL_BLOCK // CMP_BLOCK
SEL_TOPK = 16
WINDOW = 512
QCHUNK = 16
ATT_BLOCK = 128
RW_H = 4
RW_N = MIX_W // RW_H
RW_W = RW_H * RW_N
DECAY_LORA = 32
AAA_LORA = 32
GATE_LORA = 64
CONV_W = MIX_W
CONV_K = 3
DF_H = 4
DF_DQK = MIX_W // (2 * DF_H)
DF_DV = 2 * DF_DQK
MEM_LEN = 256
MX_H = 4
MX_W = MX_H * HD
PK_H = 8
N_KEYS = 128
N_EXPERTS = N_KEYS * N_KEYS
PK_DK = 128
PK_TOPK = 16
PEER_CHUNK = 128
EPS = 1e-6
GN_EPS = 64e-5
N_ALIBI = NSA_H + DF_H
ALIBI_SLOPES = tuple(2.0 ** (-8.0 * (i + 1) / N_ALIBI) for i in range(N_ALIBI))
NSA_SLOPES = ALIBI_SLOPES[0::2]
DF_SLOPES = ALIBI_SLOPES[1::2]
NSA_SPLIT = (MIX_W,) * 7 + (3 * NSA_H,)
RW_SPLIT = (RW_W, RW_W, RW_W, DECAY_LORA, AAA_LORA, GATE_LORA)
RW_P = sum(RW_SPLIT)
CONV_SPLIT = (CONV_W,) * 3
DF_SPLIT = (DF_H * 2 * DF_DQK, DF_H * 2 * DF_DQK, DF_H * DF_DV)
GROUP_SPLIT = (sum(NSA_SPLIT), RW_P, sum(CONV_SPLIT), sum(DF_SPLIT), N_BRANCH * D_MODEL)
P_TOTAL = sum(GROUP_SPLIT)

kernel_name = 'hybrid_nsa_rwkv7_conv_diffattn_peer_step'


def split_cols(z, widths):
    cuts = [int(c) for c in np.cumsum(widths)[:-1]]
    return jnp.split(z, cuts, axis=-1)


def rms_norm(x, g):
    xf = x.astype(jnp.float32)
    y = xf * lax.rsqrt(jnp.mean(xf * xf, axis=-1, keepdims=True) + EPS)
    return (y * g.astype(jnp.float32)).astype(x.dtype)


def masked_softmax(s, mask):
    s = jnp.where(mask, s, -jnp.inf)
    m = jnp.max(s, axis=-1, keepdims=True)
    m = jnp.where(jnp.isfinite(m), m, 0.0)
    e = jnp.where(mask, jnp.exp(s - m), 0.0)
    den = jnp.sum(e, axis=-1, keepdims=True)
    return e / jnp.where(den > 0, den, 1.0)


def gather_pages(pool, page_table):
    g = pool[page_table]
    return g.reshape((g.shape[0], g.shape[1] * g.shape[2]) + g.shape[3:])


def slopes_of(vals):
    return jnp.asarray(vals, jnp.float32)


def nsa_prep(nsa_p, lw):
    B, T, _ = nsa_p.shape
    q, kc, vc, ks, vs, kw, vw, ng = split_cols(nsa_p, NSA_SPLIT)
    hd = lambda z: z.reshape(B, T, NSA_H, HD)
    q = rms_norm(hd(q), lw['nsa_qn']) * (HD ** -0.5)
    ks = rms_norm(hd(ks), lw['nsa_kn'][1])
    kw = rms_norm(hd(kw), lw['nsa_kn'][2])
    gates = jax.nn.sigmoid(ng.reshape(B, T, 3, NSA_H, 1))
    return q, hd(kc), hd(vc), ks, hd(vs), kw, hd(vw), gates


def nsa_compress(rows, w_c, pe):
    B, L, H, D = rows.shape
    blk = rows.reshape(B, L // CMP_BLOCK, CMP_BLOCK, H, D) + pe[:, None, :]
    return jnp.einsum('bjihd,ide->bjhe', blk, w_c)


def cmp_attend(q, kc, vc, qpos, slopes):
    nc = kc.shape[1]
    end = jnp.arange(nc) * CMP_BLOCK + (CMP_BLOCK - 1)
    dist = (qpos[:, None] - end[None, :]).astype(jnp.float32)
    s = jnp.einsum('bthd,bjhd->bhtj', q, kc).astype(jnp.float32) - slopes[:, None, None] * dist
    p = masked_softmax(s, dist >= 0)
    o = jnp.einsum('bhtj,bjhd->bthd', p.astype(vc.dtype), vc)
    return o, p


def select_blocks(p_c, nsb, qpos, kp):
    B, H, T, _ = p_c.shape
    imp = p_c[..., :nsb * CMP_PER_SEL].reshape(B, H, T, nsb, CMP_PER_SEL).sum(-1)
    cur = qpos // SEL_BLOCK
    imp = jnp.where(jnp.arange(nsb)[None, :] < cur[:, None], imp, -1.0)
    _, idx = lax.top_k(imp, kp)
    return idx


def sel_attend(qh, kg, vg, idx, kcur, vcur, kpos_cur, qpos, slopes):
    B, H, T, K, S, D = kg.shape
    sl = slopes.reshape(1, H, 1, 1)
    cur = qpos // SEL_BLOCK
    kpos = idx[..., None] * SEL_BLOCK + jnp.arange(SEL_BLOCK)
    s1 = jnp.einsum('bhtd,bhtksd->bhtks', qh, kg).astype(jnp.float32)
    s1 = s1 - sl[..., None] * (qpos[:, None, None] - kpos).astype(jnp.float32)
    m1 = jnp.broadcast_to((idx < cur[:, None])[..., None], s1.shape)
    s2 = jnp.einsum('bhtd,bhsd->bhts', qh, kcur).astype(jnp.float32)
    s2 = s2 - sl * (qpos[:, None] - kpos_cur[None, :]).astype(jnp.float32)
    m2 = (kpos_cur[None, :] <= qpos[:, None]) & (kpos_cur[None, :] // SEL_BLOCK == cur[:, None])
    m2 = jnp.broadcast_to(m2, s2.shape)
    s = jnp.concatenate([s1.reshape(B, H, T, K * S), s2], axis=-1)
    m = jnp.concatenate([m1.reshape(B, H, T, K * S), m2], axis=-1)
    p = masked_softmax(s, m).astype(vg.dtype)
    p1 = p[..., :K * S].reshape(B, H, T, K, S)
    p2 = p[..., K * S:]
    return jnp.einsum('bhtks,bhtksd->bhtd', p1, vg) + jnp.einsum('bhts,bhsd->bhtd', p2, vcur)


def sel_prompt(q, ks, vs, idx, slopes):
    B, T, H, D = q.shape
    nsb = T // SEL_BLOCK
    nq = T // QCHUNK
    kp = idx.shape[-1]
    kb = ks.reshape(B, nsb, SEL_BLOCK, H, D).transpose(0, 3, 1, 2, 4)
    vb = vs.reshape(B, nsb, SEL_BLOCK, H, D).transpose(0, 3, 1, 2, 4)
    qc = q.reshape(B, nq, QCHUNK, H, D).transpose(1, 0, 3, 2, 4)
    ic = idx.reshape(B, H, nq, QCHUNK, kp).transpose(2, 0, 1, 3, 4)
    bi = jnp.arange(B)[:, None, None, None]
    hi = jnp.arange(H)[None, :, None, None]

    def chunk(args):
        qi, ii, c = args
        qpos = c * QCHUNK + jnp.arange(QCHUNK)
        cur = (c * QCHUNK) // SEL_BLOCK
        kcur = lax.dynamic_index_in_dim(kb, cur, axis=2, keepdims=False)
        vcur = lax.dynamic_index_in_dim(vb, cur, axis=2, keepdims=False)
        kpos_cur = cur * SEL_BLOCK + jnp.arange(SEL_BLOCK)
        return sel_attend(qi, kb[bi, hi, ii], vb[bi, hi, ii], ii, kcur, vcur, kpos_cur, qpos, slopes)

    o = lax.map(chunk, (qc, ic, jnp.arange(nq)))
    return o.transpose(1, 0, 3, 2, 4).reshape(B, T, H, D)


def sel_sample(q, ks, vs, idx, pool_k, pool_v, page_table, qpos, slopes):
    B, T, H, D = q.shape
    sub_n = PAGE_SIZE // SEL_BLOCK
    bi = jnp.arange(B)[:, None, None, None]
    hi = jnp.arange(H)[None, :, None, None]
    phys = page_table[bi, idx // sub_n]
    sub = idx % sub_n

    def gather(pool):
        pr = pool.reshape((pool.shape[0], sub_n, SEL_BLOCK) + pool.shape[2:])
        return pr[phys, sub, :, hi, :]

    o = sel_attend(q.transpose(0, 2, 1, 3), gather(pool_k), gather(pool_v), idx,
                   ks.transpose(0, 2, 1, 3), vs.transpose(0, 2, 1, 3), qpos, qpos, slopes)
    return o.transpose(0, 2, 1, 3)


def win_prompt(q, kw, vw, slopes):
    B, T, H, D = q.shape
    nb = T // ATT_BLOCK
    nprev = WINDOW // ATT_BLOCK

    def band(z):
        zp = jnp.pad(z, ((0, 0), (WINDOW, 0), (0, 0), (0, 0))).reshape(B, nb + nprev, ATT_BLOCK, H, D)
        return jnp.concatenate([zp[:, j:j + nb] for j in range(nprev + 1)], axis=2)

    kb, vb = band(kw), band(vw)
    qb = q.reshape(B, nb, ATT_BLOCK, H, D)
    qpos = jnp.arange(T).reshape(nb, ATT_BLOCK)
    kpos = (jnp.arange(nb)[:, None] - nprev) * ATT_BLOCK + jnp.arange((nprev + 1) * ATT_BLOCK)[None, :]
    dist = qpos[:, :, None] - kpos[:, None, :]
    mask = (kpos[:, None, :] >= 0) & (dist >= 0) & (dist < WINDOW)
    s = jnp.einsum('bnqhd,bnshd->bnhqs', qb, kb).astype(jnp.float32)
    s = s - slopes.reshape(1, 1, H, 1, 1) * dist[None, :, None].astype(jnp.float32)
    p = masked_softmax(s, mask[None, :, None]).astype(vb.dtype)
    return jnp.einsum('bnhqs,bnshd->bnqhd', p, vb).reshape(B, T, H, D)


def win_sample(q, kw, vw, buf_k, buf_v, qpos, slopes):
    H = q.shape[2]
    wb = buf_k.shape[1]
    kpos = jnp.concatenate([PAST_LEN - wb + jnp.arange(wb), qpos])
    dist = qpos[:, None] - kpos[None, :]
    s = jnp.concatenate([jnp.einsum('bthd,bshd->bhts', q, buf_k),
                         jnp.einsum('bthd,bshd->bhts', q, kw)], axis=-1).astype(jnp.float32)
    s = s - slopes.reshape(1, H, 1, 1) * dist.astype(jnp.float32)
    p = masked_softmax(s, (dist >= 0) & (dist < WINDOW)).astype(vw.dtype)
    return jnp.einsum('bhts,bshd->bthd', p[..., :wb], buf_v) + jnp.einsum('bhts,bshd->bthd', p[..., wb:], vw)


def nsa_combine(gates, o_c, o_s, o_w):
    B, T = o_c.shape[:2]
    y = gates[:, :, 0] * o_c + gates[:, :, 1] * o_s + gates[:, :, 2] * o_w
    return y.reshape(B, T, MIX_W)


def rwkv_mixer(rw_p, shift_prev, s0, lw):
    B, T, _ = rw_p.shape
    f32 = jnp.float32
    prev = jnp.concatenate([shift_prev[:, None, :].astype(rw_p.dtype), rw_p[:, :-1]], axis=1)
    xm = rw_p + (prev - rw_p) * lw['rw_mu']
    r, k, v, wl, al, gl = split_cols(xm, RW_SPLIT)
    w = -jax.nn.softplus(-(lw['rw_w0'] + jnp.tanh(wl) @ lw['rw_w2']).astype(f32)) - 0.5
    decay = jnp.exp(-jnp.exp(w))
    a = jax.nn.sigmoid((lw['rw_a0'] + al @ lw['rw_a2']).astype(f32))
    g = jax.nn.sigmoid(gl) @ lw['rw_g2']
    heads = lambda z: z.astype(f32).reshape(B, T, RW_H, RW_N)
    kk = heads(k * lw['rw_kk'])
    kk = kk / jnp.maximum(jnp.sqrt(jnp.sum(kk * kk, axis=-1, keepdims=True)), 1e-12)
    k_h = heads(k.astype(f32) * (1.0 + (a - 1.0) * lw['rw_ka']))
    a_h, r_h, v_h = heads(a), heads(r), heads(v)
    tm = lambda z: jnp.swapaxes(z, 0, 1)

    def step(S, inp):
        r_t, w_t, k_t, v_t, a_t, b_t = inp
        sa = jnp.einsum('bhij,bhj->bhi', S, a_t)
        S = S * w_t[:, :, None, :] + sa[..., None] * b_t[:, :, None, :] + v_t[..., None] * k_t[:, :, None, :]
        return S, jnp.einsum('bhij,bhj->bhi', S, r_t)

    s_fin, y = lax.scan(step, s0.astype(f32),
                        (tm(r_h), tm(heads(decay)), tm(k_h), tm(v_h), tm(-kk), tm(kk * a_h)))
    y = tm(y)
    mu = jnp.mean(y, axis=-1, keepdims=True)
    var = jnp.mean(jnp.square(y - mu), axis=-1, keepdims=True)
    y = ((y - mu) * lax.rsqrt(var + GN_EPS)).reshape(B, T, RW_W) * lw['rw_lnw'] + lw['rw_lnb']
    bonus = jnp.sum(r_h * k_h * lw['rw_rk'], axis=-1, keepdims=True) * v_h
    y = (y + bonus.reshape(B, T, RW_W)) * g
    return y.astype(rw_p.dtype), s_fin, rw_p[:, -1]


def conv_mixer(conv_p, prev, lw):
    bg, cg, xc = split_cols(conv_p, CONV_SPLIT)
    T = xc.shape[1]
    u = jnp.concatenate([prev.astype(xc.dtype), cg * xc], axis=1)
    y = lw['conv_b'] + sum(u[:, j:j + T] * lw['conv_w'][j] for j in range(CONV_K))
    return bg * y, u[:, T:]


def diff_prep(df_p, lw):
    B, T, _ = df_p.shape
    qd, kd, vd = split_cols(df_p, DF_SPLIT)
    q = rms_norm(qd.reshape(B, T, DF_H, 2, DF_DQK), lw['df_qn']) * (DF_DQK ** -0.5)
    k = rms_norm(kd.reshape(B, T, DF_H, 2, DF_DQK), lw['df_kn'])
    return q, k, vd.reshape(B, T, DF_H, DF_DV)


def diff_lambda(lw, l):
    lam_init = 0.8 - 0.6 * math.exp(-0.3 * l)
    f32 = jnp.float32
    lam = (jnp.exp(jnp.sum(lw['df_lq1'].astype(f32) * lw['df_lk1'].astype(f32)))
           - jnp.exp(jnp.sum(lw['df_lq2'].astype(f32) * lw['df_lk2'].astype(f32))) + lam_init)
    return lam, lam_init


def diff_prompt(q, k, v, lam, slopes):
    B, T, H = q.shape[:3]
    nb = T // ATT_BLOCK
    kpos = jnp.arange(T)
    sl = slopes.reshape(1, H, 1, 1, 1)
    qb = q.reshape((B, nb, ATT_BLOCK) + q.shape[2:]).swapaxes(0, 1)

    def block(args):
        qi, i = args
        qpos = i * ATT_BLOCK + jnp.arange(ATT_BLOCK)
        dist = (qpos[:, None] - kpos[None, :]).astype(jnp.float32)
        s = jnp.einsum('bqhcd,bshcd->bhcqs', qi, k).astype(jnp.float32) - sl * dist
        p = masked_softmax(s, dist >= 0)
        pd = (p[:, :, 0] - lam * p[:, :, 1]).astype(v.dtype)
        return jnp.einsum('bhqs,bshd->bqhd', pd, v)

    o = lax.map(block, (qb, jnp.arange(nb)))
    return o.swapaxes(0, 1).reshape(B, T, H, DF_DV)


def diff_sample(q, k, v, pool_k, pool_v, page_table, qpos, lam, slopes):
    B, T, H = q.shape[:3]
    kp = gather_pages(pool_k, page_table)
    P = kp.shape[1]
    kp = kp.reshape(B, P, H, 2, DF_DQK)
    vp = gather_pages(pool_v, page_table)
    kpos = jnp.concatenate([jnp.arange(P), qpos])
    dist = (qpos[:, None] - kpos[None, :]).astype(jnp.float32)
    s = jnp.concatenate([jnp.einsum('bqhcd,bshcd->bhcqs', q, kp),
                         jnp.einsum('bqhcd,bshcd->bhcqs', q, k)], axis=-1).astype(jnp.float32)
    s = s - slopes.reshape(1, H, 1, 1, 1) * dist
    p = masked_softmax(s, dist >= 0)
    pd = (p[:, :, 0] - lam * p[:, :, 1]).astype(v.dtype)
    return jnp.einsum('bhqs,bshd->bqhd', pd[..., :P], vp) + jnp.einsum('bhqs,bshd->bqhd', pd[..., P:], v)


def diff_out(o, lw, lam_init):
    B, T = o.shape[:2]
    return (rms_norm(o, lw['df_subln']) * (1.0 - lam_init)).reshape(B, T, MIX_W)


def merge_branches(ys, gate_p, lw):
    gs = split_cols(gate_p, (D_MODEL,) * N_BRANCH)
    z = sum(jax.nn.sigmoid(g) * (y.astype(gate_p.dtype) @ lw['w_br'][i]) for i, (y, g) in enumerate(zip(ys, gs)))
    return z @ lw['w_o']


def mem_kv(mem, lw):
    B, M, _ = mem.shape
    m = rms_norm(mem, lw['norm_mem'])
    k = rms_norm((m @ lw['w_xk']).reshape(B, M, MX_H, HD), lw['x_kn'])
    v = (m @ lw['w_xv']).reshape(B, M, MX_H, HD)
    return k, v


def cross_attend(h, mk, mv, lw):
    B, T, _ = h.shape
    q = rms_norm((h @ lw['w_xq']).reshape(B, T, MX_H, HD), lw['x_qn']) * (HD ** -0.5)
    s = jnp.einsum('bthd,bmhd->bhtm', q, mk).astype(jnp.float32)
    p = jax.nn.softmax(s, axis=-1).astype(mv.dtype)
    o = jnp.einsum('bhtm,bmhd->bthd', p, mv).reshape(B, T, MX_W)
    return o @ lw['w_xo']


def peer_tokens(xf, lw):
    n = xf.shape[0]
    q = (xf @ lw['pk_wq']).reshape(n, PK_H, 2, PK_DK // 2)
    s = jnp.einsum('nhcd,hckd->nhck', q, lw['pk_keys']).astype(jnp.float32)
    sv, si = lax.top_k(s, PK_TOPK)
    cand = sv[:, :, 0, :, None] + sv[:, :, 1, None, :]
    cv, ci = lax.top_k(cand.reshape(n, PK_H, PK_TOPK * PK_TOPK), PK_TOPK)
    i1 = jnp.take_along_axis(si[:, :, 0], ci // PK_TOPK, axis=-1)
    i2 = jnp.take_along_axis(si[:, :, 1], ci % PK_TOPK, axis=-1)
    e = i1 * N_KEYS + i2
    g = jax.nn.softmax(cv, axis=-1)
    u = jnp.take(lw['pk_u'], e, axis=0)
    vv = jnp.take(lw['pk_v'], e, axis=0)
    act = jax.nn.gelu(jnp.einsum('nhkd,nd->nhk', u, xf).astype(jnp.float32), approximate=False)
    return jnp.einsum('nhk,nhkd->nd', (g * act).astype(vv.dtype), vv)


def peer(h, lw):
    B, T, D = h.shape
    n = B * T
    flat = h.reshape(n, D)
    f = lambda xc: peer_tokens(xc, lw)
    if n % PEER_CHUNK == 0 and n > PEER_CHUNK:
        out = lax.map(f, flat.reshape(n // PEER_CHUNK, PEER_CHUNK, D)).reshape(n, D)
    else:
        out = f(flat)
    return out.reshape(B, T, D).astype(h.dtype)


def layer_tail(x, mixed, mk, mv, lw):
    x = x + mixed
    x = x + cross_attend(rms_norm(x, lw['norm_x']), mk, mv, lw)
    return x + peer(rms_norm(x, lw['norm_ffn']), lw)


def prompt_layer(x, mem, lw, l):
    B, T, _ = x.shape
    qpos = jnp.arange(T)
    h = rms_norm(x, lw['norm_mix'])
    nsa_p, rw_p, conv_p, df_p, gate_p = split_cols(h @ lw['w_in'], GROUP_SPLIT)
    ns = slopes_of(NSA_SLOPES)
    q, kc_r, vc_r, ks, vs, kw, vw, ng = nsa_prep(nsa_p, lw)
    kc = rms_norm(nsa_compress(kc_r, lw['nsa_ck_w'], lw['nsa_ck_pe']), lw['nsa_kn'][0])
    vc = nsa_compress(vc_r, lw['nsa_cv_w'], lw['nsa_cv_pe'])
    o_c, p_c = cmp_attend(q, kc, vc, qpos, ns)
    nsb = T // SEL_BLOCK
    idx = select_blocks(p_c, nsb, qpos, min(SEL_TOPK - 1, nsb))
    o_s = sel_prompt(q, ks, vs, idx, ns)
    o_w = win_prompt(q, kw, vw, ns)
    y_nsa = nsa_combine(ng, o_c, o_s, o_w)
    y_rw, s_rw, shift = rwkv_mixer(rw_p, jnp.zeros((B, RW_P), x.dtype),
                                   jnp.zeros((B, RW_H, RW_N, RW_N), jnp.float32), lw)
    y_cv, conv_st = conv_mixer(conv_p, jnp.zeros((B, CONV_K - 1, CONV_W), x.dtype), lw)
    qd, kd, vd = diff_prep(df_p, lw)
    lam, lam_init = diff_lambda(lw, l)
    y_df = diff_out(diff_prompt(qd, kd, vd, lam, slopes_of(DF_SLOPES)), lw, lam_init)
    mk, mv = mem_kv(mem, lw)
    x = layer_tail(x, merge_branches([y_nsa, y_rw, y_cv, y_df], gate_p, lw), mk, mv, lw)
    wb = min(WINDOW, T)
    return x, (kc_r, vc_r, ks, vs, kw[:, T - wb:], vw[:, T - wb:],
               kd.reshape(B, T, DF_H, 2 * DF_DQK), vd, mk, mv, s_rw, shift, conv_st)


def sample_layer(x, c, page_table, lw, l):
    B, T, _ = x.shape
    qpos = PAST_LEN + jnp.arange(T)
    h = rms_norm(x, lw['norm_mix'])
    nsa_p, rw_p, conv_p, df_p, gate_p = split_cols(h @ lw['w_in'], GROUP_SPLIT)
    ns = slopes_of(NSA_SLOPES)
    q, kc_r, vc_r, ks, vs, kw, vw, ng = nsa_prep(nsa_p, lw)
    n_new = (PAST_LEN + T) // CMP_BLOCK - PAST_LEN // CMP_BLOCK

    def cmp_rows(pool, new, w_c, pe):
        z = nsa_compress(gather_pages(pool, page_table), w_c, pe)
        if n_new > 0:
            z = jnp.concatenate([z, nsa_compress(new[:, :n_new * CMP_BLOCK], w_c, pe)], axis=1)
        return z

    kc = rms_norm(cmp_rows(c['cmp_k'], kc_r, lw['nsa_ck_w'], lw['nsa_ck_pe']), lw['nsa_kn'][0])
    vc = cmp_rows(c['cmp_v'], vc_r, lw['nsa_cv_w'], lw['nsa_cv_pe'])
    o_c, p_c = cmp_attend(q, kc, vc, qpos, ns)
    nsb = PAST_LEN // SEL_BLOCK
    idx = select_blocks(p_c, nsb, qpos, min(SEL_TOPK - 1, nsb))
    o_s = sel_sample(q, ks, vs, idx, c['sel_k'], c['sel_v'], page_table, qpos, ns)
    o_w = win_sample(q, kw, vw, c['win_k'], c['win_v'], qpos, ns)
    y_nsa = nsa_combine(ng, o_c, o_s, o_w)
    y_rw, s_rw, shift = rwkv_mixer(rw_p, c['rwkv_shift'], c['rwkv'], lw)
    y_cv, conv_st = conv_mixer(conv_p, c['conv'], lw)
    qd, kd, vd = diff_prep(df_p, lw)
    lam, lam_init = diff_lambda(lw, l)
    o_d = diff_sample(qd, kd, vd, c['diff_k'], c['diff_v'], page_table, qpos, lam, slopes_of(DF_SLOPES))
    y_df = diff_out(o_d, lw, lam_init)
    x = layer_tail(x, merge_branches([y_nsa, y_rw, y_cv, y_df], gate_p, lw), c['mem_k'], c['mem_v'], lw)
    return x, (kc_r, vc_r, ks, vs, kw, vw, kd.reshape(B, T, DF_H, 2 * DF_DQK), vd, s_rw, shift, conv_st)


def setup_inputs(seed: int = 0) -> dict:
    key = jax.random.key(seed)
    ks = iter(jax.random.split(key, 80))
    f32 = jnp.float32

    def nrm(shape, scale=1.0):
        return jax.random.normal(next(ks), shape, f32) * scale

    def gain(shape):
        return 1.0 + nrm(shape, 0.05)

    L = DEPTH
    n_pages = PAST_LEN // PAGE_SIZE
    n_used = DEC_BATCH * n_pages
    n_pool = n_used + max(1, n_used // 4)
    win_buf = min(WINDOW, PAST_LEN)
    page_table = jax.random.permutation(next(ks), n_pool)[:n_used].reshape(DEC_BATCH, n_pages).astype(jnp.int32)
    return {
        'x_prompt': nrm((BATCH, SEQ, D_MODEL)),
        'x_sample': nrm((DEC_BATCH, DEC_SEQ, D_MODEL)),
        'mem_prompt': nrm((BATCH, MEM_LEN, D_MODEL)),
        'cache_cmp_k': nrm((L, n_pool, PAGE_SIZE, NSA_H, HD)),
        'cache_cmp_v': nrm((L, n_pool, PAGE_SIZE, NSA_H, HD)),
        'cache_sel_k': nrm((L, n_pool, PAGE_SIZE, NSA_H, HD)),
        'cache_sel_v': nrm((L, n_pool, PAGE_SIZE, NSA_H, HD)),
        'cache_win_k': nrm((L, DEC_BATCH, win_buf, NSA_H, HD)),
        'cache_win_v': nrm((L, DEC_BATCH, win_buf, NSA_H, HD)),
        'cache_diff_k': nrm((L, n_pool, PAGE_SIZE, DF_H, 2 * DF_DQK)),
        'cache_diff_v': nrm((L, n_pool, PAGE_SIZE, DF_H, DF_DV)),
        'cache_mem_k': nrm((L, DEC_BATCH, MEM_LEN, MX_H, HD)),
        'cache_mem_v': nrm((L, DEC_BATCH, MEM_LEN, MX_H, HD)),
        'state_rwkv': nrm((L, DEC_BATCH, RW_H, RW_N, RW_N), 0.3),
        'state_rwkv_shift': nrm((L, DEC_BATCH, RW_P)),
        'state_conv': nrm((L, DEC_BATCH, CONV_K - 1, CONV_W)),
        'page_table': page_table,
        'norm_mix': gain((L, D_MODEL)),
        'w_in': nrm((L, D_MODEL, P_TOTAL), D_MODEL ** -0.5),
        'nsa_qn': gain((L, HD)),
        'nsa_kn': gain((L, 3, HD)),
        'nsa_ck_w': nrm((L, CMP_BLOCK, HD, HD), (CMP_BLOCK * HD) ** -0.5),
        'nsa_ck_pe': nrm((L, CMP_BLOCK, HD), 0.1),
        'nsa_cv_w': nrm((L, CMP_BLOCK, HD, HD), (CMP_BLOCK * HD) ** -0.5),
        'nsa_cv_pe': nrm((L, CMP_BLOCK, HD), 0.1),
        'rw_mu': jax.random.uniform(next(ks), (L, RW_P), f32),
        'rw_w0': nrm((L, RW_W), 0.5),
        'rw_w2': nrm((L, DECAY_LORA, RW_W), 0.1 * DECAY_LORA ** -0.5),
        'rw_a0': nrm((L, RW_W), 0.1),
        'rw_a2': nrm((L, AAA_LORA, RW_W), 0.1 * AAA_LORA ** -0.5),
        'rw_g2': nrm((L, GATE_LORA, RW_W), GATE_LORA ** -0.5),
        'rw_kk': gain((L, RW_W)),
        'rw_ka': gain((L, RW_W)),
        'rw_rk': nrm((L, RW_H, RW_N), 0.1),
        'rw_lnw': gain((L, RW_W)),
        'rw_lnb': nrm((L, RW_W), 0.01),
        'conv_w': nrm((L, CONV_K, CONV_W), CONV_K ** -0.5),
        'conv_b': nrm((L, CONV_W), 0.01),
        'df_qn': gain((L, DF_DQK)),
        'df_kn': gain((L, DF_DQK)),
        'df_lq1': nrm((L, DF_DQK), 0.1),
        'df_lk1': nrm((L, DF_DQK), 0.1),
        'df_lq2': nrm((L, DF_DQK), 0.1),
        'df_lk2': nrm((L, DF_DQK), 0.1),
        'df_subln': gain((L, DF_DV)),
        'w_br': nrm((L, N_BRANCH, MIX_W, D_MODEL), MIX_W ** -0.5),
        'w_o': nrm((L, D_MODEL, D_MODEL), D_MODEL ** -0.5),
        'norm_x': gain((L, D_MODEL)),
        'norm_mem': gain((L, D_MODEL)),
        'w_xq': nrm((L, D_MODEL, MX_W), D_MODEL ** -0.5),
        'w_xk': nrm((L, D_MODEL, MX_W), D_MODEL ** -0.5),
        'w_xv': nrm((L, D_MODEL, MX_W), D_MODEL ** -0.5),
        'x_qn': gain((L, HD)),
        'x_kn': gain((L, HD)),
        'w_xo': nrm((L, MX_W, D_MODEL), MX_W ** -0.5),
        'norm_ffn': gain((L, D_MODEL)),
        'pk_wq': nrm((L, D_MODEL, PK_H * PK_DK), D_MODEL ** -0.5),
        'pk_keys': nrm((L, PK_H, 2, N_KEYS, PK_DK // 2), (PK_DK // 2) ** -0.5),
        'pk_u': nrm((L, N_EXPERTS, D_MODEL), D_MODEL ** -0.5),
        'pk_v': nrm((L, N_EXPERTS, D_MODEL), PK_H ** -0.5),
    }


def reference(x_prompt, x_sample, mem_prompt, cache_cmp_k, cache_cmp_v, cache_sel_k, cache_sel_v,
              cache_win_k, cache_win_v, cache_diff_k, cache_diff_v, cache_mem_k, cache_mem_v,
              state_rwkv, state_rwkv_shift, state_conv, page_table,
              norm_mix, w_in, nsa_qn, nsa_kn, nsa_ck_w, nsa_ck_pe, nsa_cv_w, nsa_cv_pe,
              rw_mu, rw_w0, rw_w2, rw_a0, rw_a2, rw_g2, rw_kk, rw_ka, rw_rk, rw_lnw, rw_lnb,
              conv_w, conv_b, df_qn, df_kn, df_lq1, df_lk1, df_lq2, df_lk2, df_subln,
              w_br, w_o, norm_x, norm_mem, w_xq, w_xk, w_xv, x_qn, x_kn, w_xo,
              norm_ffn, pk_wq, pk_keys, pk_u, pk_v):
    xp, xs = x_prompt, x_sample
    p_acc = [[] for _ in range(13)]
    s_acc = [[] for _ in range(11)]
    for l in range(DEPTH):
        lw = dict(norm_mix=norm_mix[l], w_in=w_in[l], nsa_qn=nsa_qn[l], nsa_kn=nsa_kn[l],
                  nsa_ck_w=nsa_ck_w[l], nsa_ck_pe=nsa_ck_pe[l], nsa_cv_w=nsa_cv_w[l], nsa_cv_pe=nsa_cv_pe[l],
                  rw_mu=rw_mu[l], rw_w0=rw_w0[l], rw_w2=rw_w2[l], rw_a0=rw_a0[l], rw_a2=rw_a2[l],
                  rw_g2=rw_g2[l], rw_kk=rw_kk[l], rw_ka=rw_ka[l], rw_rk=rw_rk[l], rw_lnw=rw_lnw[l],
                  rw_lnb=rw_lnb[l], conv_w=conv_w[l], conv_b=conv_b[l], df_qn=df_qn[l], df_kn=df_kn[l],
                  df_lq1=df_lq1[l], df_lk1=df_lk1[l], df_lq2=df_lq2[l], df_lk2=df_lk2[l],
                  df_subln=df_subln[l], w_br=w_br[l], w_o=w_o[l], norm_x=norm_x[l], norm_mem=norm_mem[l],
                  w_xq=w_xq[l], w_xk=w_xk[l], w_xv=w_xv[l], x_qn=x_qn[l], x_kn=x_kn[l], w_xo=w_xo[l],
                  norm_ffn=norm_ffn[l], pk_wq=pk_wq[l], pk_keys=pk_keys[l], pk_u=pk_u[l], pk_v=pk_v[l])
        c = dict(cmp_k=cache_cmp_k[l], cmp_v=cache_cmp_v[l], sel_k=cache_sel_k[l], sel_v=cache_sel_v[l],
                 win_k=cache_win_k[l], win_v=cache_win_v[l], diff_k=cache_diff_k[l], diff_v=cache_diff_v[l],
                 mem_k=cache_mem_k[l], mem_v=cache_mem_v[l], rwkv=state_rwkv[l],
                 rwkv_shift=state_rwkv_shift[l], conv=state_conv[l])
        xp, ps = prompt_layer(xp, mem_prompt, lw, l)
        xs, ss = sample_layer(xs, c, page_table, lw, l)
        for acc, st in zip(p_acc, ps):
            acc.append(st)
        for acc, st in zip(s_acc, ss):
            acc.append(st)
    (p_cmp_k, p_cmp_v, p_sel_k, p_sel_v, p_win_k, p_win_v, p_diff_k, p_diff_v,
     p_mem_k, p_mem_v, p_rwkv, p_rwkv_shift, p_conv) = [jnp.stack(a) for a in p_acc]
    (s_cmp_k, s_cmp_v, s_sel_k, s_sel_v, s_win_k, s_win_v, s_diff_k, s_diff_v,
     s_rwkv, s_rwkv_shift, s_conv) = [jnp.stack(a) for a in s_acc]
    return (xp, xs, p_cmp_k, p_cmp_v, p_sel_k, p_sel_v, p_win_k, p_win_v, p_diff_k, p_diff_v,
            p_mem_k, p_mem_v, p_rwkv, p_rwkv_shift, p_conv,
            s_cmp_k, s_cmp_v, s_sel_k, s_sel_v, s_win_k, s_win_v, s_diff_k, s_diff_v,
            s_rwkv, s_rwkv_shift, s_conv)
```

```python
import functools
import math

import jax
import jax.numpy as jnp
import numpy as np
from jax import lax
from jax.experimental import pallas as pl
from jax.experimental.pallas import tpu as pltpu

F32 = jnp.float32
BF16 = jnp.bfloat16
I32 = jnp.int32

D_MODEL = 1024
PAGE_SIZE = 128
N_BRANCH = 4
MIX_W = 256
NSA_H = 4
HD = 64
CMP_BLOCK = 32
SEL_BLOCK = 64
SEL_TOPK = 16
WINDOW = 512
RW_H = 4
RW_N = 64
DECAY_LORA = 32
AAA_LORA = 32
GATE_LORA = 64
RW_P = 896
CONV_K = 3
DF_H = 4
DF_DQK = 32
DF_DV = 64
MX_H = 4
PK_H = 8
N_KEYS = 128
PK_DK = 128
PK_TOPK = 16
EPS = 1e-6
GN_EPS = 64e-5
N_ALIBI = NSA_H + DF_H
ALIBI_SLOPES = tuple(2.0 ** (-8.0 * (i + 1) / N_ALIBI) for i in range(N_ALIBI))
NSA_SLOPES = ALIBI_SLOPES[0::2]
DF_SLOPES = ALIBI_SLOPES[1::2]

LANES = 128
VMEM_LIMIT = 56 * 1024 * 1024

OFF_GATE = 0
OFF_VD = 4096
OFF_NG = 4352
OFF_RW = 4480
OFF_Q = 5376
OFF_KS = 5632
OFF_KW = 5888
OFF_QD = 6144
OFF_KD = 6400
OFF_KC = 6656
OFF_VC = 6912
OFF_VS = 7168
OFF_VW = 7424
OFF_BG = 7680
OFF_CG = 7936
OFF_XC = 8192
P_PAD = 8448
ZB_KS, ZB_KW, ZB_KD, ZB_VS, ZB_VW, ZB_VD = range(6)


def _dot(a, b):
    return jnp.dot(a, b, preferred_element_type=F32)


def _dot_nt(a, b):
    return lax.dot_general(a, b, (((1,), (1,)), ((), ())), preferred_element_type=F32)


def _seg_sum(x, m):
    hi = x.astype(BF16)
    lo = (x - hi.astype(F32)).astype(BF16)
    return _dot(hi, m) + _dot(lo, m)


def _iota(shape, dim):
    return lax.broadcasted_iota(I32, shape, dim)


def _row_tile(n, pref):
    t = pref
    while t > 8 and n % t:
        t //= 2
    assert n % t == 0, (n, pref)
    return t


def _cparams(sem):
    return pltpu.CompilerParams(dimension_semantics=sem, vmem_limit_bytes=VMEM_LIMIT)


def _vmem_full():
    return pl.BlockSpec(memory_space=pltpu.VMEM)


_NORMED = ((OFF_Q, 64), (OFF_KS, 64), (OFF_KW, 64), (OFF_QD, 32), (OFF_KD, 32))
_ZB_SRC = ((OFF_KS, ZB_KS), (OFF_KW, ZB_KW), (OFF_KD, ZB_KD), (OFF_VS, ZB_VS), (OFF_VW, ZB_VW), (OFF_VD, ZB_VD))


def _proj_plan():
    zb_of = dict(_ZB_SRC)
    normed = dict(_NORMED)
    special = sorted(set(zb_of) | set(normed))
    plan, c0 = [], 0
    for s in special + [P_PAD]:
        while c0 < s:
            w = min(512, s - c0)
            plan.append((c0, w, 0, -1))
            c0 += w
        if s < P_PAD:
            plan.append((s, 256, normed.get(s, 0), zb_of.get(s, -1)))
            c0 = s + 256
    return plan


def _proj_kernel(x_ref, g_ref, w_ref, gains_ref, m64_ref, m32_ref, z_ref, zb_ref):
    x = x_ref[...]
    h = x * lax.rsqrt(jnp.mean(x * x, axis=-1, keepdims=True) + EPS) * g_ref[...]
    hb = h.astype(BF16)
    norm_row = {o: r for r, (o, _) in enumerate(_NORMED)}
    for c0, w, gs, slot in _proj_plan():
        z = _dot(hb, w_ref[:, c0:c0 + w])
        if gs:
            m = m64_ref[...] if gs == 64 else m32_ref[...]
            ss = _seg_sum(z * z, m)
            r = norm_row[c0]
            z = z * lax.rsqrt(ss * (1.0 / gs) + EPS) * gains_ref[r:r + 1, :]
        z_ref[:, c0:c0 + w] = z
        if slot >= 0:
            zb_ref[:, slot * 256:(slot + 1) * 256] = z.astype(BF16)


def _proj(x2d, g, w_pad, gains, m64, m32):
    n = x2d.shape[0]
    tm = _row_tile(n, 256)
    return pl.pallas_call(
        _proj_kernel,
        grid=(n // tm,),
        in_specs=[pl.BlockSpec((tm, D_MODEL), lambda i: (i, 0)),
                  pl.BlockSpec((1, D_MODEL), lambda i: (0, 0)),
                  _vmem_full(),
                  pl.BlockSpec((8, 256), lambda i: (0, 0)),
                  pl.BlockSpec((256, 256), lambda i: (0, 0)),
                  pl.BlockSpec((256, 256), lambda i: (0, 0))],
        out_specs=[pl.BlockSpec((tm, P_PAD), lambda i: (i, 0)),
                   pl.BlockSpec((tm, 1536), lambda i: (i, 0))],
        out_shape=[jax.ShapeDtypeStruct((n, P_PAD), F32), jax.ShapeDtypeStruct((n, 1536), BF16)],
        compiler_params=_cparams(("parallel",)),
    )(x2d, g, w_pad, gains, m64, m32)


CMP_K = CMP_BLOCK * 256


def _cmp_kernel(kc_ref, vc_ref, wk_ref, wv_ref, pek_ref, pev_ref, m64_ref, gk_ref,
                kce_ref, kco_ref, vce_ref, vco_ref):
    for src, w_ref, pe_ref, outs, is_k in ((kc_ref, wk_ref, pek_ref, (kce_ref, kco_ref), True),
                                           (vc_ref, wv_ref, pev_ref, (vce_ref, vco_ref), False)):
        for par in range(2):
            rows = src[:, par * CMP_K:(par + 1) * CMP_K] + pe_ref[...]
            acc = _dot(rows.astype(BF16), w_ref[...])
            if is_k:
                ss = _seg_sum(acc * acc, m64_ref[...])
                acc = acc * lax.rsqrt(ss * (1.0 / HD) + EPS) * gk_ref[...]
            outs[par][...] = acc.astype(BF16)


def _cmp(kc2, vc2, wk_bd, wv_bd, pek, pev, m64, gk):
    nr = kc2.shape[0]
    rb = _row_tile(nr, 64)
    out = jax.ShapeDtypeStruct((nr, 256), BF16)
    ospec = pl.BlockSpec((rb, 256), lambda i: (i, 0))
    ispec = pl.BlockSpec((rb, 2 * CMP_K), lambda i: (i, 0))
    par = lambda s: pl.BlockSpec(s, lambda i: (0,) * len(s))
    return pl.pallas_call(
        _cmp_kernel,
        grid=(nr // rb,),
        in_specs=[ispec, ispec, par((CMP_K, 256)), par((CMP_K, 256)), par((1, CMP_K)), par((1, CMP_K)),
                  par((256, 256)), par((1, 256))],
        out_specs=[ospec, ospec, ospec, ospec],
        out_shape=[out, out, out, out],
        compiler_params=_cparams(("parallel",)),
    )(kc2, vc2, wk_bd, wv_bd, pek, pev, m64, gk)


def _softmax_rows(s):
    m = jnp.max(s, axis=-1, keepdims=True)
    e = jnp.exp(s - m)
    return e / jnp.sum(e, axis=-1, keepdims=True)


def _nsa_kernel(q_ref, ng_ref, kce_ref, kco_ref, vce_ref, vco_ref, ks_ref, vs_ref, kw_ref, vw_ref, ex_ref,
                o_ref, *, t_len, tq, kp, band):
    i = pl.program_id(1)
    nsb = t_len // SEL_BLOCK
    q = q_ref[...]
    sig = jax.nn.sigmoid(ng_ref[...])
    lane = _iota((1, 256), 1)
    qpos = i * tq + _iota((tq, 1), 0)
    cur = jnp.right_shift(qpos, 6)
    jb = _iota((1, nsb), 1)
    kpos = _iota((1, t_len), 1)
    dist = (qpos - kpos).astype(F32)
    curf = jnp.where(jnp.right_shift(kpos, 6) == cur, jnp.where(kpos <= qpos, 1.0, 0.0), 0.0)
    de = (qpos - (jb * SEL_BLOCK + (CMP_BLOCK - 1))).astype(F32)
    do = (qpos - (jb * SEL_BLOCK + (SEL_BLOCK - 1))).astype(F32)
    past = jb < cur
    start = jnp.clip(i * tq - WINDOW, 0, t_len - band)
    start = pl.multiple_of(start, tq)
    kposw = start + _iota((1, band), 1)
    dw = qpos - kposw
    okw = jnp.where(dw >= 0, jnp.where(dw < WINDOW, 1.0, 0.0), 0.0) > 0.5
    dwf = dw.astype(F32)
    kwb = kw_ref[pl.ds(start, band), :]
    vwb = vw_ref[pl.ds(start, band), :]
    acc = jnp.zeros((tq, 256), F32)
    for h in range(NSA_H):
        hm = jnp.right_shift(lane, 6) == h
        qh = jnp.where(hm, q, 0.0).astype(BF16)
        sl = NSA_SLOPES[h]
        se = jnp.where(de >= 0, _dot_nt(qh, kce_ref[...]) - sl * de, -jnp.inf)
        so = jnp.where(do >= 0, _dot_nt(qh, kco_ref[...]) - sl * do, -jnp.inf)
        m = jnp.maximum(jnp.max(se, axis=-1, keepdims=True), jnp.max(so, axis=-1, keepdims=True))
        m = jnp.where(m > -jnp.inf, m, 0.0)
        pe = jnp.where(de >= 0, jnp.exp(se - m), 0.0)
        po = jnp.where(do >= 0, jnp.exp(so - m), 0.0)
        den = jnp.sum(pe, axis=-1, keepdims=True) + jnp.sum(po, axis=-1, keepdims=True)
        den = jnp.where(den > 0, den, 1.0)
        pe = pe / den
        po = po / den
        o_c = _dot(pe.astype(BF16), vce_ref[...]) + _dot(po.astype(BF16), vco_ref[...])
        imp = jnp.where(past, pe + po, -1.0)
        rank = jnp.zeros((tq, nsb), F32)
        for j2 in range(nsb):
            col = imp[:, j2:j2 + 1]
            before = jnp.where(jb > j2, 1.0, 0.0)
            rank = rank + jnp.where(col > imp, 1.0, jnp.where(col == imp, before, 0.0))
        member = jnp.where(past, jnp.where(rank < kp, 1.0, 0.0), 0.0)
        allowed = (_dot(member.astype(BF16), ex_ref[...]) + curf) > 0.5
        s = jnp.where(allowed, _dot_nt(qh, ks_ref[...]) - sl * dist, -jnp.inf)
        o_s = _dot(_softmax_rows(s).astype(BF16), vs_ref[...])
        s = jnp.where(okw, _dot_nt(qh, kwb) - sl * dwf, -jnp.inf)
        o_w = _dot(_softmax_rows(s).astype(BF16), vwb)
        y = (sig[:, h:h + 1] * o_c + sig[:, NSA_H + h:NSA_H + h + 1] * o_s
             + sig[:, 2 * NSA_H + h:2 * NSA_H + h + 1] * o_w)
        acc = acc + jnp.where(hm, y, 0.0)
    o_ref[...] = acc


def _nsa_prompt(z, zb, cmp4, expand, bsz, t_len):
    tq = 128
    nt = t_len // tq
    nsb = t_len // SEL_BLOCK
    kp = min(SEL_TOPK - 1, nsb)
    band = min(WINDOW + tq, t_len)
    kv = lambda c: pl.BlockSpec((t_len, 256), lambda b, i: (b, c))
    cs = pl.BlockSpec((nsb, 256), lambda b, i: (b, 0))
    return pl.pallas_call(
        functools.partial(_nsa_kernel, t_len=t_len, tq=tq, kp=kp, band=band),
        grid=(bsz, nt),
        in_specs=[pl.BlockSpec((tq, 256), lambda b, i: (b * nt + i, OFF_Q // 256)),
                  pl.BlockSpec((tq, 128), lambda b, i: (b * nt + i, OFF_NG // 128)),
                  cs, cs, cs, cs, kv(ZB_KS), kv(ZB_VS), kv(ZB_KW), kv(ZB_VW),
                  pl.BlockSpec((nsb, t_len), lambda b, i: (0, 0))],
        out_specs=pl.BlockSpec((tq, 256), lambda b, i: (b * nt + i, 0)),
        out_shape=jax.ShapeDtypeStruct((bsz * t_len, 256), F32),
        compiler_params=_cparams(("parallel", "parallel")),
    )(z, z, *cmp4, zb, zb, zb, zb, expand)


def _diff_kernel(lam_ref, q_ref, k_ref, v_ref, m64_ref, gain_ref, o_ref, *, t_len, tq):
    i = pl.program_id(1)
    lam = lam_ref[0]
    q = q_ref[...]
    lane = _iota((1, 256), 1)
    qpos = i * tq + _iota((tq, 1), 0)
    kpos = _iota((1, t_len), 1)
    causal = kpos <= qpos
    dist = (qpos - kpos).astype(F32)
    acc = jnp.zeros((tq, 256), F32)
    for h in range(DF_H):
        ps = []
        for c in range(2):
            lm = jnp.right_shift(lane, 5) == (2 * h + c)
            qh = jnp.where(lm, q, 0.0).astype(BF16)
            s = jnp.where(causal, _dot_nt(qh, k_ref[...]) - DF_SLOPES[h] * dist, -jnp.inf)
            ps.append(_softmax_rows(s))
        pd = (ps[0] - lam * ps[1]).astype(BF16)
        acc = acc + jnp.where(jnp.right_shift(lane, 6) == h, _dot(pd, v_ref[...]), 0.0)
    ss = _seg_sum(acc * acc, m64_ref[...])
    o_ref[...] = acc * lax.rsqrt(ss * (1.0 / DF_DV) + EPS) * gain_ref[...]


def _diff_prompt(lam, z, zb, m64, gain, bsz, t_len):
    tq = 128
    nt = t_len // tq
    kv = lambda c: pl.BlockSpec((t_len, 256), lambda b, i: (b, c))
    return pl.pallas_call(
        functools.partial(_diff_kernel, t_len=t_len, tq=tq),
        grid=(bsz, nt),
        in_specs=[pl.BlockSpec(memory_space=pltpu.SMEM),
                  pl.BlockSpec((tq, 256), lambda b, i: (b * nt + i, OFF_QD // 256)),
                  kv(ZB_KD), kv(ZB_VD),
                  pl.BlockSpec((256, 256), lambda b, i: (0, 0)),
                  pl.BlockSpec((1, 256), lambda b, i: (0, 0))],
        out_specs=pl.BlockSpec((tq, 256), lambda b, i: (b * nt + i, 0)),
        out_shape=jax.ShapeDtypeStruct((bsz * t_len, 256), F32),
        compiler_params=_cparams(("parallel", "parallel")),
    )(lam, z, zb, zb, m64, gain)


def _shift_rows(x, first_rows, k):
    n = x.shape[0]
    rolled = pltpu.roll(x, k, 0)
    row = _iota((n, 1), 0)
    out = rolled
    for r in range(k):
        out = jnp.where(row == r, first_rows[8 - k + r:8 - k + r + 1, :], out)
    return out


def _seq_kernel(rw_ref, rwp_ref, bg_ref, cg_ref, xc_ref, cgp_ref, xcp_ref, sh0_ref, cv0_ref,
                mu_ref, w0_ref, a0_ref, lw_ref, la_ref, lg_ref, kk_ref, ka_ref, cw_ref, cb_ref, m64_ref,
                r_ref, w_ref, k_ref, v_ref, am_ref, bb_ref, g_ref, ycv_ref, ulast_ref, *, ts):
    i = pl.program_id(1)
    x = rw_ref[...]
    first = jnp.where(i == 0, sh0_ref[0], rwp_ref[...])
    prev = _shift_rows(x, first, 1)
    xm = x + (prev - x) * mu_ref[...]
    r = xm[:, 0:256]
    k = xm[:, 256:512]
    v = xm[:, 512:768]
    lo = xm[:, 768:896]
    wlin = w0_ref[...] + _dot(jnp.tanh(lo).astype(BF16), lw_ref[...])
    w = -jax.nn.softplus(-wlin) - 0.5
    decay = jnp.exp(-jnp.exp(w))
    a = jax.nn.sigmoid(a0_ref[...] + _dot(lo.astype(BF16), la_ref[...]))
    g = _dot(jax.nn.sigmoid(lo).astype(BF16), lg_ref[...])
    kk = k * kk_ref[...]
    nrm = jnp.sqrt(_seg_sum(kk * kk, m64_ref[...]))
    kk = kk / jnp.maximum(nrm, 1e-12)
    r_ref[...] = r
    w_ref[...] = decay
    k_ref[...] = k * (1.0 + (a - 1.0) * ka_ref[...])
    v_ref[...] = v
    am_ref[...] = -kk
    bb_ref[...] = kk * a
    g_ref[...] = g
    u = cg_ref[...] * xc_ref[...]
    ufirst = jnp.where(i == 0, cv0_ref[0], cgp_ref[...] * xcp_ref[...])
    u1 = _shift_rows(u, ufirst, 1)
    u2 = _shift_rows(u, ufirst, 2)
    y = cb_ref[...] + u2 * cw_ref[0:1, :] + u1 * cw_ref[1:2, :] + u * cw_ref[2:3, :]
    ycv_ref[...] = bg_ref[...] * y
    ulast_ref[0] = u[ts - 8:ts, :]


def _seq(z, shift0, conv0, lwp, bsz, t_len):
    ts = _row_tile(t_len, 512)
    nt = t_len // ts
    n = bsz * t_len
    row = lambda w, c: pl.BlockSpec((ts, w), lambda b, i: (b * nt + i, c))
    prv = lambda w, c: pl.BlockSpec((8, w), lambda b, i: (jnp.maximum((b * nt + i) * (ts // 8) - 1, 0), c))
    par = lambda s: pl.BlockSpec(s, lambda b, i: (0,) * len(s))
    o256 = pl.BlockSpec((ts, 256), lambda b, i: (b * nt + i, 0))
    sd = jax.ShapeDtypeStruct((n, 256), F32)
    return pl.pallas_call(
        functools.partial(_seq_kernel, ts=ts),
        grid=(bsz, nt),
        in_specs=[row(RW_P, OFF_RW // RW_P), prv(RW_P, OFF_RW // RW_P),
                  row(256, OFF_BG // 256), row(256, OFF_CG // 256), row(256, OFF_XC // 256),
                  prv(256, OFF_CG // 256), prv(256, OFF_XC // 256),
                  pl.BlockSpec((1, 8, RW_P), lambda b, i: (b, 0, 0)),
                  pl.BlockSpec((1, 8, 256), lambda b, i: (b, 0, 0)),
                  par((1, RW_P)), par((1, 256)), par((1, 256)), par((128, 256)), par((128, 256)), par((128, 256)),
                  par((1, 256)), par((1, 256)), par((8, 256)), par((1, 256)), par((256, 256))],
        out_specs=[o256] * 8 + [pl.BlockSpec((1, 8, 256), lambda b, i: (b, 0, 0))],
        out_shape=[sd] * 8 + [jax.ShapeDtypeStruct((bsz, 8, 256), F32)],
        compiler_params=_cparams(("parallel", "arbitrary")),
    )(z, z, z, z, z, z, z, shift0, conv0, lwp['mu'], lwp['w0'], lwp['a0'], lwp['lw'], lwp['la'], lwp['lg'],
      lwp['kk'], lwp['ka'], lwp['cw'], lwp['cb'], lwp['m64'])


def _scan_kernel(s0_ref, am_ref, w_ref, bb_ref, k_ref, r_ref, v_ref, y_ref, s_ref, *, tc, ng, ih):
    @pl.when(pl.program_id(0) == 0)
    def _():
        s_ref[...] = s0_ref[...]

    def step(t, carry):
        def group(g, c2):
            h = g // ih
            hs = pl.multiple_of(h * RW_N, RW_N)
            st = s_ref[g]
            a_c = am_ref[t, pl.ds(hs, RW_N), :]
            sa = jnp.sum(st * a_c, axis=0, keepdims=True)
            vrow = v_ref[t, pl.ds(g, 1), :]
            st = (st * w_ref[t, pl.ds(hs, RW_N), :] + sa * bb_ref[t, pl.ds(hs, RW_N), :]
                  + vrow * k_ref[t, pl.ds(hs, RW_N), :])
            s_ref[g] = st
            y_ref[t, pl.ds(g, 1), :] = jnp.sum(st * r_ref[t, pl.ds(hs, RW_N), :], axis=0, keepdims=True)
            return c2

        return lax.fori_loop(0, ng, group, carry)

    lax.fori_loop(0, tc, step, 0)


def _scan(s0, am, w, bb, k, r, v, t_len, ih):
    ng = RW_H * ih
    tc = _row_tile(t_len, 16)
    col = pl.BlockSpec((tc, 256, LANES), lambda i: (i, 0, 0))
    rowv = pl.BlockSpec((tc, ng, LANES), lambda i: (i, 0, 0))
    st = pl.BlockSpec((ng, RW_N, LANES), lambda i: (0, 0, 0))
    return pl.pallas_call(
        functools.partial(_scan_kernel, tc=tc, ng=ng, ih=ih),
        grid=(t_len // tc,),
        in_specs=[st, col, col, col, col, col, rowv],
        out_specs=[rowv, st],
        out_shape=[jax.ShapeDtypeStruct((t_len, ng, LANES), F32), jax.ShapeDtypeStruct((ng, RW_N, LANES), F32)],
        compiler_params=_cparams(("arbitrary",)),
    )(s0, am, w, bb, k, r, v)


def _to_cols(x, bsz, t_len, il):
    x = x.reshape(bsz, t_len, 256).transpose(1, 2, 0)
    return jnp.tile(x, (1, 1, il))


def _to_rows(x, bsz, t_len, il, ih):
    x = x.reshape(bsz, t_len, RW_H, ih, il).transpose(1, 2, 3, 4, 0)
    return x.reshape(t_len, RW_H * ih, il * bsz)


def _from_rows(y, bsz, t_len, il, ih):
    y = y.reshape(t_len, RW_H, ih, il, bsz).transpose(4, 0, 1, 2, 3)
    return y.reshape(bsz * t_len, 256)


def _state_in(s, bsz, il, ih):
    s = s.reshape(bsz, RW_H, ih, il, RW_N).transpose(1, 2, 4, 3, 0)
    return s.reshape(RW_H * ih, RW_N, il * bsz)


def _state_out(s, bsz, il, ih):
    s = s.reshape(RW_H, ih, RW_N, il, bsz).transpose(4, 0, 1, 3, 2)
    return s.reshape(bsz, RW_H, RW_N, RW_N)


def _rwkv_scan(seq_out, s0, bsz, t_len):
    r, w, k, v, am, bb = seq_out
    il = LANES // bsz
    ih = RW_N // il
    cols = [_to_cols(a, bsz, t_len, il) for a in (am, w, bb, k, r)]
    y, s_fin = _scan(_state_in(s0, bsz, il, ih), *cols, _to_rows(v, bsz, t_len, il, ih), t_len, ih)
    return _from_rows(y, bsz, t_len, il, ih), _state_out(s_fin, bsz, il, ih)


def _merge_kernel(x_ref, gate_ref, ynsa_ref, yrw_ref, r_ref, k_ref, v_ref, g_ref, ycv_ref, ydf_ref,
                  wbr_ref, wo_ref, lnw_ref, lnb_ref, rk_ref, m64_ref, nx_ref, wxq_ref, qg_ref,
                  x1_ref, qx_ref):
    m64 = m64_ref[...]
    y = yrw_ref[...]
    mu = _seg_sum(y, m64) * (1.0 / RW_N)
    d = y - mu
    var = _seg_sum(d * d, m64) * (1.0 / RW_N)
    yn = d * lax.rsqrt(var + GN_EPS) * lnw_ref[...] + lnb_ref[...]
    v = v_ref[...]
    bonus = _seg_sum(r_ref[...] * k_ref[...] * rk_ref[...], m64) * v
    y_rw = (yn + bonus) * g_ref[...]
    ys = (ynsa_ref[...], y_rw, ycv_ref[...], ydf_ref[...])
    zsum = None
    for b in range(N_BRANCH):
        t = jax.nn.sigmoid(gate_ref[:, b * D_MODEL:(b + 1) * D_MODEL]) * _dot(ys[b].astype(BF16), wbr_ref[b])
        zsum = t if zsum is None else zsum + t
    x1 = x_ref[...] + _dot(zsum.astype(BF16), wo_ref[...])
    x1_ref[...] = x1
    hx = x1 * lax.rsqrt(jnp.mean(x1 * x1, axis=-1, keepdims=True) + EPS) * nx_ref[...]
    qx = _dot(hx.astype(BF16), wxq_ref[...])
    ss = _seg_sum(qx * qx, m64)
    qx_ref[...] = qx * lax.rsqrt(ss * (1.0 / HD) + EPS) * qg_ref[...]


def _merge(x2d, z, ynsa, yrw, r, k, v, g, ycv, ydf, mw):
    n = x2d.shape[0]
    tm = _row_tile(n, 256)
    row = lambda w, c=0: pl.BlockSpec((tm, w), lambda i: (i, c))
    par = lambda s: pl.BlockSpec(s, lambda i: (0,) * len(s))
    return pl.pallas_call(
        _merge_kernel,
        grid=(n // tm,),
        in_specs=[row(D_MODEL), row(4 * D_MODEL, OFF_GATE)] + [row(256)] * 8
                 + [par((4, 256, D_MODEL)), par((D_MODEL, D_MODEL)), par((1, 256)), par((1, 256)), par((1, 256)),
                    par((256, 256)), par((1, D_MODEL)), par((D_MODEL, 256)), par((1, 256))],
        out_specs=[row(D_MODEL), row(256)],
        out_shape=[jax.ShapeDtypeStruct((n, D_MODEL), F32), jax.ShapeDtypeStruct((n, 256), F32)],
        compiler_params=_cparams(("parallel",)),
    )(x2d, z, ynsa, yrw, r, k, v, g, ycv, ydf, mw['wbr'], mw['wo'], mw['lnw'], mw['lnb'], mw['rk'], mw['m64'],
      mw['nx'], mw['wxq'], mw['qg'])


def _memkv_kernel(x_ref, g_ref, w_ref, m64_ref, kg_ref, k_ref, v_ref):
    x = x_ref[...]
    h = (x * lax.rsqrt(jnp.mean(x * x, axis=-1, keepdims=True) + EPS) * g_ref[...]).astype(BF16)
    k = _dot(h, w_ref[:, 0:256])
    ss = _seg_sum(k * k, m64_ref[...])
    k_ref[...] = k * lax.rsqrt(ss * (1.0 / HD) + EPS) * kg_ref[...]
    v_ref[...] = _dot(h, w_ref[:, 256:512])


def _memkv(mem2d, g, wkv, m64, kg):
    n = mem2d.shape[0]
    tm = _row_tile(n, 256)
    par = lambda s: pl.BlockSpec(s, lambda i: (0,) * len(s))
    row = lambda w: pl.BlockSpec((tm, w), lambda i: (i, 0))
    sd = jax.ShapeDtypeStruct((n, 256), F32)
    return pl.pallas_call(
        _memkv_kernel,
        grid=(n // tm,),
        in_specs=[row(D_MODEL), par((1, D_MODEL)), par((D_MODEL, 512)), par((256, 256)), par((1, 256))],
        out_specs=[row(256), row(256)],
        out_shape=[sd, sd],
        compiler_params=_cparams(("parallel",)),
    )(mem2d, g, wkv, m64, kg)


def _xattn_kernel(qx_ref, mk_ref, mv_ref, x1_ref, wxo_ref, nf_ref, x2_ref, hf_ref):
    q = qx_ref[...]
    lane = _iota((1, 256), 1)
    mk = mk_ref[0]
    mv = mv_ref[0]
    acc = jnp.zeros(q.shape, F32)
    for h in range(MX_H):
        hm = jnp.right_shift(lane, 6) == h
        s = _dot_nt(jnp.where(hm, q, 0.0).astype(BF16), mk)
        acc = acc + jnp.where(hm, _dot(_softmax_rows(s).astype(BF16), mv), 0.0)
    x2 = x1_ref[...] + _dot(acc.astype(BF16), wxo_ref[...])
    x2_ref[...] = x2
    hf_ref[...] = x2 * lax.rsqrt(jnp.mean(x2 * x2, axis=-1, keepdims=True) + EPS) * nf_ref[...]


def _xattn(qx, mk, mv, x1, wxo, nf, bsz, t_len):
    tq = _row_tile(t_len, 256)
    nt = t_len // tq
    mlen = mk.shape[1]
    row = lambda w: pl.BlockSpec((tq, w), lambda b, i: (b * nt + i, 0))
    mem = pl.BlockSpec((1, mlen, 256), lambda b, i: (b, 0, 0))
    sd = jax.ShapeDtypeStruct((bsz * t_len, D_MODEL), F32)
    return pl.pallas_call(
        _xattn_kernel,
        grid=(bsz, nt),
        in_specs=[row(256), mem, mem, row(D_MODEL),
                  pl.BlockSpec((256, D_MODEL), lambda b, i: (0, 0)),
                  pl.BlockSpec((1, D_MODEL), lambda b, i: (0, 0))],
        out_specs=[row(D_MODEL), row(D_MODEL)],
        out_shape=[sd, sd],
        compiler_params=_cparams(("parallel", "parallel")),
    )(qx, mk, mv, x1, wxo, nf)


PEER_TILE = 128


def _topk_cols(s, ids, n_take, payload=None):
    big = s.shape[0]
    vals, picks = [], []
    for _ in range(n_take):
        m = jnp.max(s, axis=0, keepdims=True)
        ix = jnp.min(jnp.where(s == m, ids, big), axis=0, keepdims=True)
        hit = ids == ix
        vals.append(m)
        picks.append(ix if payload is None else jnp.max(jnp.where(hit, payload, -1), axis=0, keepdims=True))
        s = jnp.where(hit, -jnp.inf, s)
    return jnp.concatenate(vals, axis=0), jnp.concatenate(picks, axis=0)


def _peer_score_kernel(hf_ref, wq_ref, kbd_ref, e_ref, g_ref):
    q = _dot(hf_ref[...].astype(BF16), wq_ref[...])
    st = _dot_nt(kbd_ref[...], q.astype(BF16))
    kid = _iota((N_KEYS, PEER_TILE), 0)
    cid = _iota((PK_TOPK * PK_TOPK, PEER_TILE), 0)
    for h in range(PK_H):
        sv0, si0 = _topk_cols(st[(2 * h) * N_KEYS:(2 * h + 1) * N_KEYS, :], kid, PK_TOPK)
        sv1, si1 = _topk_cols(st[(2 * h + 1) * N_KEYS:(2 * h + 2) * N_KEYS, :], kid, PK_TOPK)
        cand = jnp.concatenate([sv0[a:a + 1, :] + sv1 for a in range(PK_TOPK)], axis=0)
        ecand = jnp.concatenate([si0[a:a + 1, :] * N_KEYS + si1 for a in range(PK_TOPK)], axis=0)
        cv, ev = _topk_cols(cand, cid, PK_TOPK, payload=ecand)
        ex = jnp.exp(cv - cv[0:1, :])
        g_ref[h * PK_TOPK:(h + 1) * PK_TOPK, :] = ex / jnp.sum(ex, axis=0, keepdims=True)
        e_ref[h * PK_TOPK:(h + 1) * PK_TOPK, :] = ev


def _peer_score(hf, wq, kbd):
    n = hf.shape[0]
    nt = n // PEER_TILE
    nsel = PK_H * PK_TOPK
    out = pl.BlockSpec((nsel, PEER_TILE), lambda i: (i, 0))
    return pl.pallas_call(
        _peer_score_kernel,
        grid=(nt,),
        in_specs=[pl.BlockSpec((PEER_TILE, D_MODEL), lambda i: (i, 0)),
                  pl.BlockSpec((D_MODEL, PK_H * PK_DK), lambda i: (0, 0)),
                  pl.BlockSpec((PK_H * 2 * N_KEYS, PK_H * PK_DK), lambda i: (0, 0))],
        out_specs=[out, out],
        out_shape=[jax.ShapeDtypeStruct((nt * nsel, PEER_TILE), I32),
                   jax.ShapeDtypeStruct((nt * nsel, PEER_TILE), F32)],
        compiler_params=_cparams(("parallel",)),
    )(hf, wq, kbd)


def _unpack_pair(wd):
    lo = pltpu.bitcast(jnp.left_shift(wd, 16), F32)
    hi = pltpu.bitcast(jnp.bitwise_and(wd, jnp.uint32(0xFFFF0000)), F32)
    return lo, hi


def _peer_u_kernel(idx_ref, x_ref, tab_ref, act_ref, xv_ref, a_ref):
    nsel = PK_H * PK_TOPK
    sub = _iota((8, LANES), 0)

    def token(n, carry):
        x8 = x_ref[n]
        xr = pltpu.roll(x8, 4, 0)
        xv_ref[0] = jnp.where(sub < 4, x8, 0.0)
        xv_ref[1] = jnp.where(sub < 4, xr, 0.0)
        xv_ref[2] = jnp.where(sub >= 4, xr, 0.0)
        xv_ref[3] = jnp.where(sub >= 4, x8, 0.0)

        def sel(j, c2):
            e = idx_ref[0, 0, n * nsel + j]
            par = jnp.bitwise_and(e, 1)
            lo, hi = _unpack_pair(tab_ref[jnp.right_shift(e, 1)])
            p = lo * xv_ref[2 * par] + hi * xv_ref[2 * par + 1]
            a_ref[pl.ds(j, 1), :] = jnp.sum(p, axis=0, keepdims=True)
            return c2

        lax.fori_loop(0, nsel, sel, 0, unroll=8)
        act_ref[pl.ds(n, 1), :] = jnp.sum(a_ref[...].T, axis=0, keepdims=True)
        return carry

    lax.fori_loop(0, PEER_TILE, token, 0)


def _peer_u(idx3, hf8, tab):
    nt = idx3.shape[0]
    nsel = PK_H * PK_TOPK
    return pl.pallas_call(
        _peer_u_kernel,
        grid=(nt,),
        in_specs=[pl.BlockSpec((1, 1, PEER_TILE * nsel), lambda i: (i, 0, 0), memory_space=pltpu.SMEM),
                  pl.BlockSpec((PEER_TILE, 8, LANES), lambda i: (i, 0, 0)),
                  _vmem_full()],
        out_specs=pl.BlockSpec((PEER_TILE, nsel), lambda i: (i, 0)),
        out_shape=jax.ShapeDtypeStruct((nt * PEER_TILE, nsel), F32),
        scratch_shapes=[pltpu.VMEM((4, 8, LANES), F32), pltpu.VMEM((nsel, LANES), F32)],
        compiler_params=_cparams(("parallel",)),
    )(idx3, hf8, tab)


def _peer_v_kernel(idx_ref, c_ref, x_ref, tab_ref, o_ref, mk_ref):
    nsel = PK_H * PK_TOPK
    sub = _iota((8, LANES), 0)
    mk_ref[0] = jnp.where(sub < 4, 1.0, 0.0)
    mk_ref[1] = jnp.where(sub >= 4, 1.0, 0.0)

    def token(n, carry):
        def sel(j, acc):
            e = idx_ref[0, 0, n * nsel + j]
            cm = mk_ref[jnp.bitwise_and(e, 1)] * c_ref[0, 0, n * nsel + j]
            lo, hi = _unpack_pair(tab_ref[jnp.right_shift(e, 1)])
            return acc[0] + cm * lo, acc[1] + cm * hi

        z = jnp.zeros((8, LANES), F32)
        alo, ahi = lax.fori_loop(0, nsel, sel, (z, z), unroll=8)
        alo = alo + pltpu.roll(alo, 4, 0)
        ahi = ahi + pltpu.roll(ahi, 4, 0)
        o_ref[n] = x_ref[n] + jnp.where(sub < 4, alo, ahi)
        return carry

    lax.fori_loop(0, PEER_TILE, token, 0)


def _peer_v(idx3, c3, x8, tab):
    nt = idx3.shape[0]
    nsel = PK_H * PK_TOPK
    sm = pl.BlockSpec((1, 1, PEER_TILE * nsel), lambda i: (i, 0, 0), memory_space=pltpu.SMEM)
    xs = pl.BlockSpec((PEER_TILE, 8, LANES), lambda i: (i, 0, 0))
    return pl.pallas_call(
        _peer_v_kernel,
        grid=(nt,),
        in_specs=[sm, sm, xs, _vmem_full()],
        out_specs=xs,
        out_shape=jax.ShapeDtypeStruct(x8.shape, F32),
        scratch_shapes=[pltpu.VMEM((2, 8, LANES), F32)],
        compiler_params=_cparams(("parallel",)),
    )(idx3, c3, x8, tab)


def _pack_table(t):
    e = t.shape[0]
    b = lax.bitcast_convert_type(t.astype(BF16), jnp.uint16).astype(jnp.uint32)
    w = b[:, :512] | (b[:, 512:] << 16)
    return w.reshape(e // 2, 8, LANES)


def _peer(hf, x2, pw):
    n = hf.shape[0]
    nt = n // PEER_TILE
    nsel = PK_H * PK_TOPK
    e_t, g_t = _peer_score(hf, pw['wq'], pw['kbd'])
    tok_major = lambda a: a.reshape(nt, nsel, PEER_TILE).transpose(0, 2, 1)
    e_tm = tok_major(e_t)
    idx3 = e_tm.reshape(nt, 1, PEER_TILE * nsel)
    act = _peer_u(idx3, hf.reshape(n, 8, LANES), pw['u'])
    c = tok_major(g_t).reshape(n, nsel) * jax.nn.gelu(act, approximate=False)
    out8 = _peer_v(idx3, c.reshape(nt, 1, PEER_TILE * nsel), x2.reshape(n, 8, LANES), pw['v'])
    return out8.reshape(n, D_MODEL)


def _masked_softmax(s, mask):
    s = jnp.where(mask, s, -jnp.inf)
    m = jnp.max(s, axis=-1, keepdims=True)
    m = jnp.where(jnp.isfinite(m), m, 0.0)
    e = jnp.where(mask, jnp.exp(s - m), 0.0)
    den = jnp.sum(e, axis=-1, keepdims=True)
    return e / jnp.where(den > 0, den, 1.0)


def _gather_pages(pool, page_table):
    g = pool[page_table]
    return g.reshape((g.shape[0], g.shape[1] * g.shape[2]) + g.shape[3:])


def _rms(x, g):
    return x * lax.rsqrt(jnp.mean(x * x, axis=-1, keepdims=True) + EPS) * g


def _compress(rows, w_c, pe):
    b, l, h, d = rows.shape
    blk = rows.reshape(b, l // CMP_BLOCK, CMP_BLOCK, h, d) + pe[:, None, :]
    return jnp.einsum('bjihd,ide->bjhe', blk, w_c)


def _sample_attn(zs, c, page_table, lw, lam, lam_init, bsz, t_len, past_len):
    col = lambda off, w=256: zs[:, off:off + w].reshape(bsz, t_len, -1)
    hd = lambda a: a.reshape(bsz, t_len, NSA_H, HD)
    q, ks, kw = hd(col(OFF_Q)), hd(col(OFF_KS)), hd(col(OFF_KW))
    kc_r, vc_r, vs, vw = hd(col(OFF_KC)), hd(col(OFF_VC)), hd(col(OFF_VS)), hd(col(OFF_VW))
    gates = jax.nn.sigmoid(col(OFF_NG, 128)[..., :3 * NSA_H].reshape(bsz, t_len, 3, NSA_H, 1))
    qpos = past_len + jnp.arange(t_len)
    ns = jnp.asarray(NSA_SLOPES, F32)
    n_new = (past_len + t_len) // CMP_BLOCK - past_len // CMP_BLOCK

    def cmp_rows(pool, new, w_c, pe):
        zc = _compress(_gather_pages(pool, page_table), w_c, pe)
        if n_new > 0:
            zc = jnp.concatenate([zc, _compress(new[:, :n_new * CMP_BLOCK], w_c, pe)], axis=1)
        return zc

    kc = _rms(cmp_rows(c['cmp_k'], kc_r, lw['nsa_ck_w'], lw['nsa_ck_pe']), lw['nsa_kn'][0])
    vc = cmp_rows(c['cmp_v'], vc_r, lw['nsa_cv_w'], lw['nsa_cv_pe'])
    nc = kc.shape[1]
    end = jnp.arange(nc) * CMP_BLOCK + (CMP_BLOCK - 1)
    dist = (qpos[:, None] - end[None, :]).astype(F32)
    s = jnp.einsum('bthd,bjhd->bhtj', q, kc).astype(F32) - ns[:, None, None] * dist
    p_c = _masked_softmax(s, dist >= 0)
    o_c = jnp.einsum('bhtj,bjhd->bthd', p_c, vc)
    nsb = past_len // SEL_BLOCK
    kp = min(SEL_TOPK - 1, nsb)
    imp = p_c[..., :nsb * 2].reshape(bsz, NSA_H, t_len, nsb, 2).sum(-1)
    cur = qpos // SEL_BLOCK
    imp = jnp.where(jnp.arange(nsb)[None, :] < cur[:, None], imp, -1.0)
    _, idx = lax.top_k(imp, kp)
    sub_n = PAGE_SIZE // SEL_BLOCK
    bi = jnp.arange(bsz)[:, None, None, None]
    hi = jnp.arange(NSA_H)[None, :, None, None]
    phys = page_table[bi, idx // sub_n]
    sub = idx % sub_n

    def gather(pool):
        pr = pool.reshape((pool.shape[0], sub_n, SEL_BLOCK) + pool.shape[2:])
        return pr[phys, sub, :, hi, :]

    qh = q.transpose(0, 2, 1, 3)
    kg, vg = gather(c['sel_k']), gather(c['sel_v'])
    kcur, vcur = ks.transpose(0, 2, 1, 3), vs.transpose(0, 2, 1, 3)
    sl = ns.reshape(1, NSA_H, 1, 1)
    kpos = idx[..., None] * SEL_BLOCK + jnp.arange(SEL_BLOCK)
    s1 = jnp.einsum('bhtd,bhtksd->bhtks', qh, kg).astype(F32)
    s1 = s1 - sl[..., None] * (qpos[:, None, None] - kpos).astype(F32)
    m1 = jnp.broadcast_to((idx < cur[:, None])[..., None], s1.shape)
    s2 = jnp.einsum('bhtd,bhsd->bhts', qh, kcur).astype(F32)
    s2 = s2 - sl * (qpos[:, None] - qpos[None, :]).astype(F32)
    m2 = (qpos[None, :] <= qpos[:, None]) & (qpos[None, :] // SEL_BLOCK == cur[:, None])
    m2 = jnp.broadcast_to(m2, s2.shape)
    k_s = kp * SEL_BLOCK
    sa = jnp.concatenate([s1.reshape(bsz, NSA_H, t_len, k_s), s2], axis=-1)
    ma = jnp.concatenate([m1.reshape(bsz, NSA_H, t_len, k_s), m2], axis=-1)
    p = _masked_softmax(sa, ma)
    o_s = (jnp.einsum('bhtks,bhtksd->bhtd', p[..., :k_s].reshape(bsz, NSA_H, t_len, kp, SEL_BLOCK), vg)
           + jnp.einsum('bhts,bhsd->bhtd', p[..., k_s:], vcur)).transpose(0, 2, 1, 3)
    buf_k, buf_v = c['win_k'], c['win_v']
    wb = buf_k.shape[1]
    kposw = jnp.concatenate([past_len - wb + jnp.arange(wb), qpos])
    dw = qpos[:, None] - kposw[None, :]
    s = jnp.concatenate([jnp.einsum('bthd,bshd->bhts', q, buf_k),
                         jnp.einsum('bthd,bshd->bhts', q, kw)], axis=-1).astype(F32)
    s = s - ns.reshape(1, NSA_H, 1, 1) * dw.astype(F32)
    p = _masked_softmax(s, (dw >= 0) & (dw < WINDOW))
    o_w = jnp.einsum('bhts,bshd->bthd', p[..., :wb], buf_v) + jnp.einsum('bhts,bshd->bthd', p[..., wb:], vw)
    y_nsa = (gates[:, :, 0] * o_c + gates[:, :, 1] * o_s + gates[:, :, 2] * o_w).reshape(bsz * t_len, MIX_W)
    qd = col(OFF_QD).reshape(bsz, t_len, DF_H, 2, DF_DQK)
    kd = col(OFF_KD).reshape(bsz, t_len, DF_H, 2, DF_DQK)
    vd = col(OFF_VD).reshape(bsz, t_len, DF_H, DF_DV)
    kpg = _gather_pages(c['diff_k'], page_table)
    plen = kpg.shape[1]
    kpg = kpg.reshape(bsz, plen, DF_H, 2, DF_DQK)
    vpg = _gather_pages(c['diff_v'], page_table)
    kposd = jnp.concatenate([jnp.arange(plen), qpos])
    dd = (qpos[:, None] - kposd[None, :]).astype(F32)
    s = jnp.concatenate([jnp.einsum('bqhcd,bshcd->bhcqs', qd, kpg),
                         jnp.einsum('bqhcd,bshcd->bhcqs', qd, kd)], axis=-1).astype(F32)
    s = s - jnp.asarray(DF_SLOPES, F32).reshape(1, DF_H, 1, 1, 1) * dd
    p = _masked_softmax(s, dd >= 0)
    pd = p[:, :, 0] - lam * p[:, :, 1]
    o_d = jnp.einsum('bhqs,bshd->bqhd', pd[..., :plen], vpg) + jnp.einsum('bhqs,bshd->bqhd', pd[..., plen:], vd)
    y_df = (_rms(o_d, lw['df_subln']) * (1.0 - lam_init)).reshape(bsz * t_len, MIX_W)
    return y_nsa, y_df


def _block_ones(gs):
    i = np.arange(256)
    return jnp.asarray((i[:, None] // gs) == (i[None, :] // gs), BF16)


def _tile_heads(v, reps):
    return jnp.tile(v.astype(F32), reps).reshape(1, -1)


def _prep_layer(lw, l):
    m64, m32 = _block_ones(64), _block_ones(32)
    w_in = lw['w_in']
    src = np.cumsum([0, 256, 256, 256, 256, 256, 256, 256, 12, RW_P, 256, 256, 256, 256, 256, 256, 4096])
    names = ['q', 'kc', 'vc', 'ks', 'vs', 'kw', 'vw', 'ng', 'rw', 'bg', 'cg', 'xc', 'qd', 'kd', 'vd', 'gate']
    dst = dict(q=OFF_Q, kc=OFF_KC, vc=OFF_VC, ks=OFF_KS, vs=OFF_VS, kw=OFF_KW, vw=OFF_VW, ng=OFF_NG, rw=OFF_RW,
               bg=OFF_BG, cg=OFF_CG, xc=OFF_XC, qd=OFF_QD, kd=OFF_KD, vd=OFF_VD, gate=OFF_GATE)
    order = sorted(names, key=lambda nm: dst[nm])
    pieces = []
    pos = 0
    for nm in order:
        k = names.index(nm)
        assert dst[nm] == pos, (nm, dst[nm], pos)
        piece = w_in[:, int(src[k]):int(src[k + 1])]
        if nm == 'ng':
            piece = jnp.pad(piece, ((0, 0), (0, 128 - 3 * NSA_H)))
        pieces.append(piece)
        pos += piece.shape[1]
    assert pos == P_PAD
    w_pad = jnp.concatenate(pieces, axis=1).astype(BF16)
    gains = jnp.concatenate([
        _tile_heads(lw['nsa_qn'], 4) * (HD ** -0.5), _tile_heads(lw['nsa_kn'][1], 4), _tile_heads(lw['nsa_kn'][2], 4),
        _tile_heads(lw['df_qn'], 8) * (DF_DQK ** -0.5), _tile_heads(lw['df_kn'], 8),
        jnp.zeros((3, 256), F32)], axis=0)
    eye4 = jnp.eye(NSA_H, dtype=F32)
    bd = lambda w: jnp.einsum('hg,ide->ihdge', eye4, w).reshape(CMP_K, 256).astype(BF16)
    lam_init = 0.8 - 0.6 * math.exp(-0.3 * l)
    lam = (jnp.exp(jnp.sum(lw['df_lq1'] * lw['df_lk1'])) - jnp.exp(jnp.sum(lw['df_lq2'] * lw['df_lk2'])) + lam_init)
    lora = lambda w, a, b: jnp.zeros((128, 256), F32).at[a:b].set(w).astype(BF16)
    seqw = dict(mu=lw['rw_mu'].reshape(1, RW_P), w0=lw['rw_w0'].reshape(1, 256), a0=lw['rw_a0'].reshape(1, 256),
                lw=lora(lw['rw_w2'], 0, 32), la=lora(lw['rw_a2'], 32, 64), lg=lora(lw['rw_g2'], 64, 128),
                kk=lw['rw_kk'].reshape(1, 256), ka=lw['rw_ka'].reshape(1, 256),
                cw=jnp.pad(lw['conv_w'], ((0, 8 - CONV_K), (0, 0))), cb=lw['conv_b'].reshape(1, 256), m64=m64)
    mw = dict(wbr=lw['w_br'].astype(BF16), wo=lw['w_o'].astype(BF16), lnw=lw['rw_lnw'].reshape(1, 256),
              lnb=lw['rw_lnb'].reshape(1, 256), rk=lw['rw_rk'].reshape(1, 256), m64=m64,
              nx=lw['norm_x'].reshape(1, D_MODEL), wxq=lw['w_xq'].astype(BF16),
              qg=_tile_heads(lw['x_qn'], 4) * (HD ** -0.5))
    keys = lw['pk_keys'].reshape(PK_H * 2, N_KEYS, PK_DK // 2)
    eye16 = jnp.eye(PK_H * 2, dtype=F32)
    kbd = jnp.einsum('gf,gkd->gkfd', eye16, keys).reshape(PK_H * 2 * N_KEYS, PK_H * PK_DK).astype(BF16)
    pw = dict(wq=lw['pk_wq'].astype(BF16), kbd=kbd, u=_pack_table(lw['pk_u']), v=_pack_table(lw['pk_v']))
    return dict(
        g_mix=lw['norm_mix'].reshape(1, D_MODEL), w_pad=w_pad, gains=gains, m64=m64, m32=m32,
        wk_bd=bd(lw['nsa_ck_w']), wv_bd=bd(lw['nsa_cv_w']),
        pek=jnp.tile(lw['nsa_ck_pe'], (1, NSA_H)).reshape(1, CMP_K),
        pev=jnp.tile(lw['nsa_cv_pe'], (1, NSA_H)).reshape(1, CMP_K),
        gk0=_tile_heads(lw['nsa_kn'][0], 4), lam=lam.reshape(1).astype(F32), lam_init=lam_init,
        df_gain=_tile_heads(lw['df_subln'], 4) * (1.0 - lam_init), seqw=seqw, mw=mw, pw=pw,
        g_mem=lw['norm_mem'].reshape(1, D_MODEL),
        wkv=jnp.concatenate([lw['w_xk'], lw['w_xv']], axis=1).astype(BF16),
        kg=_tile_heads(lw['x_kn'], 4), wxo=lw['w_xo'].astype(BF16), nf=lw['norm_ffn'].reshape(1, D_MODEL))


def _pad_rows(x, mult):
    n = x.shape[0]
    p = (-n) % mult
    return x if p == 0 else jnp.pad(x, ((0, p),) + ((0, 0),) * (x.ndim - 1))


def _tail(x2d, z, ys, rw, mk, mv, pl_, bsz, t_len):
    y_nsa, y_scan, y_cv, y_df = ys
    r, k, v, g = rw
    x1, qx = _merge(x2d, z, y_nsa, y_scan, r, k, v, g, y_cv, y_df, pl_['mw'])
    x2, hf = _xattn(qx, mk, mv, x1, pl_['wxo'], pl_['nf'], bsz, t_len)
    n = x2.shape[0]
    out = _peer(_pad_rows(hf, PEER_TILE), _pad_rows(x2, PEER_TILE), pl_['pw'])
    return out[:n]


def _prompt_layer(x2d, mem2d, pl_, bsz, t_len):
    n = bsz * t_len
    z, zb = _proj(x2d, pl_['g_mix'], pl_['w_pad'], pl_['gains'], pl_['m64'], pl_['m32'])
    kc_raw, vc_raw = z[:, OFF_KC:OFF_KC + 256], z[:, OFF_VC:OFF_VC + 256]
    cmp4 = _cmp(kc_raw.reshape(n // SEL_BLOCK, 2 * CMP_K), vc_raw.reshape(n // SEL_BLOCK, 2 * CMP_K),
                pl_['wk_bd'], pl_['wv_bd'], pl_['pek'], pl_['pev'], pl_['m64'], pl_['gk0'])
    nsb = t_len // SEL_BLOCK
    expand = jnp.asarray(np.arange(nsb)[:, None] == (np.arange(t_len)[None, :] // SEL_BLOCK), BF16)
    y_nsa = _nsa_prompt(z, zb, cmp4, expand, bsz, t_len)
    y_df = _diff_prompt(pl_['lam'], z, zb, pl_['m64'], pl_['df_gain'], bsz, t_len)
    seq = _seq(z, jnp.zeros((bsz, 8, RW_P), F32), jnp.zeros((bsz, 8, 256), F32), pl_['seqw'], bsz, t_len)
    r, w, k, v, am, bb, g, y_cv, ulast = seq
    y_scan, s_fin = _rwkv_scan((r, w, k, v, am, bb), jnp.zeros((bsz, RW_H, RW_N, RW_N), F32), bsz, t_len)
    mk, mv = _memkv(mem2d, pl_['g_mem'], pl_['wkv'], pl_['m64'], pl_['kg'])
    mlen = mem2d.shape[0] // bsz
    mk3, mv3 = mk.reshape(bsz, mlen, 256), mv.reshape(bsz, mlen, 256)
    x_out = _tail(x2d, z, (y_nsa, y_scan, y_cv, y_df), (r, k, v, g), mk3.astype(BF16), mv3.astype(BF16),
                  pl_, bsz, t_len)
    st = lambda off: z[:, off:off + 256].reshape(bsz, t_len, NSA_H, HD)
    wb = min(WINDOW, t_len)
    shift = z[:, OFF_RW:OFF_RW + RW_P].reshape(bsz, t_len, RW_P)[:, -1]
    states = (st(OFF_KC), st(OFF_VC), st(OFF_KS), st(OFF_VS), st(OFF_KW)[:, t_len - wb:], st(OFF_VW)[:, t_len - wb:],
              st(OFF_KD), st(OFF_VD), mk3.reshape(bsz, mlen, MX_H, HD), mv3.reshape(bsz, mlen, MX_H, HD),
              s_fin, shift, ulast[:, 8 - (CONV_K - 1):])
    return x_out, states


def _sample_layer(x2d, c, page_table, lw, pl_, bsz, t_len, past_len):
    z, _ = _proj(x2d, pl_['g_mix'], pl_['w_pad'], pl_['gains'], pl_['m64'], pl_['m32'])
    y_nsa, y_df = _sample_attn(z, c, page_table, lw, pl_['lam'][0], pl_['lam_init'], bsz, t_len, past_len)
    shift0 = jnp.zeros((bsz, 8, RW_P), F32).at[:, 7].set(c['rwkv_shift'])
    conv0 = jnp.zeros((bsz, 8, 256), F32).at[:, 8 - (CONV_K - 1):].set(c['conv'])
    seq = _seq(z, shift0, conv0, pl_['seqw'], bsz, t_len)
    r, w, k, v, am, bb, g, y_cv, ulast = seq
    y_scan, s_fin = _rwkv_scan((r, w, k, v, am, bb), c['rwkv'], bsz, t_len)
    mk = c['mem_k'].reshape(bsz, -1, 256).astype(BF16)
    mv = c['mem_v'].reshape(bsz, -1, 256).astype(BF16)
    x_out = _tail(x2d, z, (y_nsa, y_scan, y_cv, y_df), (r, k, v, g), mk, mv, pl_, bsz, t_len)
    st = lambda off: z[:, off:off + 256].reshape(bsz, t_len, NSA_H, HD)
    shift = z[:, OFF_RW:OFF_RW + RW_P].reshape(bsz, t_len, RW_P)[:, -1]
    states = (st(OFF_KC), st(OFF_VC), st(OFF_KS), st(OFF_VS), st(OFF_KW), st(OFF_VW), st(OFF_KD), st(OFF_VD),
              s_fin, shift, ulast[:, 8 - (CONV_K - 1):])
    return x_out, states


def kernel(x_prompt, x_sample, mem_prompt, cache_cmp_k, cache_cmp_v, cache_sel_k, cache_sel_v, cache_win_k, cache_win_v, cache_diff_k, cache_diff_v, cache_mem_k, cache_mem_v, state_rwkv, state_rwkv_shift, state_conv, page_table, norm_mix, w_in, nsa_qn, nsa_kn, nsa_ck_w, nsa_ck_pe, nsa_cv_w, nsa_cv_pe, rw_mu, rw_w0, rw_w2, rw_a0, rw_a2, rw_g2, rw_kk, rw_ka, rw_rk, rw_lnw, rw_lnb, conv_w, conv_b, df_qn, df_kn, df_lq1, df_lk1, df_lq2, df_lk2, df_subln, w_br, w_o, norm_x, norm_mem, w_xq, w_xk, w_xv, x_qn, x_kn, w_xo, norm_ffn, pk_wq, pk_keys, pk_u, pk_v):
    bp, tp, _ = x_prompt.shape
    bs, ts, _ = x_sample.shape
    depth = w_in.shape[0]
    past_len = page_table.shape[1] * PAGE_SIZE
    weights = dict(norm_mix=norm_mix, w_in=w_in, nsa_qn=nsa_qn, nsa_kn=nsa_kn, nsa_ck_w=nsa_ck_w,
                   nsa_ck_pe=nsa_ck_pe, nsa_cv_w=nsa_cv_w, nsa_cv_pe=nsa_cv_pe, rw_mu=rw_mu, rw_w0=rw_w0,
                   rw_w2=rw_w2, rw_a0=rw_a0, rw_a2=rw_a2, rw_g2=rw_g2, rw_kk=rw_kk, rw_ka=rw_ka, rw_rk=rw_rk,
                   rw_lnw=rw_lnw, rw_lnb=rw_lnb, conv_w=conv_w, conv_b=conv_b, df_qn=df_qn, df_kn=df_kn,
                   df_lq1=df_lq1, df_lk1=df_lk1, df_lq2=df_lq2, df_lk2=df_lk2, df_subln=df_subln, w_br=w_br,
                   w_o=w_o, norm_x=norm_x, norm_mem=norm_mem, w_xq=w_xq, w_xk=w_xk, w_xv=w_xv, x_qn=x_qn,
                   x_kn=x_kn, w_xo=w_xo, norm_ffn=norm_ffn, pk_wq=pk_wq, pk_keys=pk_keys, pk_u=pk_u, pk_v=pk_v)
    xp = x_prompt.reshape(bp * tp, D_MODEL)
    xs = x_sample.reshape(bs * ts, D_MODEL)
    mem2d = mem_prompt.reshape(-1, D_MODEL)
    p_acc, s_acc = [], []
    for l in range(depth):
        lw = {k_: v_[l] for k_, v_ in weights.items()}
        pl_ = _prep_layer(lw, l)
        c = dict(cmp_k=cache_cmp_k[l], cmp_v=cache_cmp_v[l], sel_k=cache_sel_k[l], sel_v=cache_sel_v[l],
                 win_k=cache_win_k[l], win_v=cache_win_v[l], diff_k=cache_diff_k[l], diff_v=cache_diff_v[l],
                 mem_k=cache_mem_k[l], mem_v=cache_mem_v[l], rwkv=state_rwkv[l],
                 rwkv_shift=state_rwkv_shift[l], conv=state_conv[l])
        xp, ps = _prompt_layer(xp, mem2d, pl_, bp, tp)
        xs, ss = _sample_layer(xs, c, page_table, lw, pl_, bs, ts, past_len)
        p_acc.append(ps)
        s_acc.append(ss)
    p_out = [jnp.stack(a) for a in zip(*p_acc)]
    s_out = [jnp.stack(a) for a in zip(*s_acc)]
    return (xp.reshape(bp, tp, D_MODEL), xs.reshape(bs, ts, D_MODEL), *p_out, *s_out)
```

```python
import functools
import math

import jax
import jax.numpy as jnp
import numpy as np
from jax import lax
from jax.experimental import pallas as pl
from jax.experimental.pallas import tpu as pltpu

F32 = jnp.float32
BF16 = jnp.bfloat16
I32 = jnp.int32

D_MODEL = 1024
PAGE_SIZE = 128
N_BRANCH = 4
MIX_W = 256
NSA_H = 4
HD = 64
CMP_BLOCK = 32
SEL_BLOCK = 64
SEL_TOPK = 16
WINDOW = 512
RW_H = 4
RW_N = 64
DECAY_LORA = 32
AAA_LORA = 32
GATE_LORA = 64
RW_P = 896
CONV_K = 3
DF_H = 4
DF_DQK = 32
DF_DV = 64
MX_H = 4
PK_H = 8
N_KEYS = 128
PK_DK = 128
PK_TOPK = 16
EPS = 1e-6
GN_EPS = 64e-5
N_ALIBI = NSA_H + DF_H
ALIBI_SLOPES = tuple(2.0 ** (-8.0 * (i + 1) / N_ALIBI) for i in range(N_ALIBI))
NSA_SLOPES = ALIBI_SLOPES[0::2]
DF_SLOPES = ALIBI_SLOPES[1::2]

LANES = 128
VMEM_LIMIT = 56 * 1024 * 1024

OFF_GATE = 0
OFF_VD = 4096
OFF_NG = 4352
OFF_RW = 4480
OFF_Q = 5376
OFF_KS = 5632
OFF_KW = 5888
OFF_QD = 6144
OFF_KD = 6400
OFF_KC = 6656
OFF_VC = 6912
OFF_VS = 7168
OFF_VW = 7424
OFF_BG = 7680
OFF_CG = 7936
OFF_XC = 8192
P_PAD = 8448
ZB_KS, ZB_KW, ZB_KD, ZB_VS, ZB_VW, ZB_VD = range(6)


def _dot(a, b):
    return jnp.dot(a, b, preferred_element_type=F32)


def _dot_nt(a, b):
    return lax.dot_general(a, b, (((1,), (1,)), ((), ())), preferred_element_type=F32)


def _seg_sum(x, m):
    hi = x.astype(BF16)
    lo = (x - hi.astype(F32)).astype(BF16)
    return _dot(hi, m) + _dot(lo, m)


def _iota(shape, dim):
    return lax.broadcasted_iota(I32, shape, dim)


def _row_tile(n, pref):
    t = pref
    while t > 8 and n % t:
        t //= 2
    assert n % t == 0, (n, pref)
    return t


def _cparams(sem):
    return pltpu.CompilerParams(dimension_semantics=sem, vmem_limit_bytes=VMEM_LIMIT)


def _vmem_full():
    return pl.BlockSpec(memory_space=pltpu.VMEM)


_NORMED = ((OFF_Q, 64), (OFF_KS, 64), (OFF_KW, 64), (OFF_QD, 32), (OFF_KD, 32))
_ZB_SRC = ((OFF_KS, ZB_KS), (OFF_KW, ZB_KW), (OFF_KD, ZB_KD), (OFF_VS, ZB_VS), (OFF_VW, ZB_VW), (OFF_VD, ZB_VD))


def _proj_plan():
    zb_of = dict(_ZB_SRC)
    normed = dict(_NORMED)
    special = sorted(set(zb_of) | set(normed))
    plan, c0 = [], 0
    for s in special + [P_PAD]:
        while c0 < s:
            w = min(512, s - c0)
            plan.append((c0, w, 0, -1))
            c0 += w
        if s < P_PAD:
            plan.append((s, 256, normed.get(s, 0), zb_of.get(s, -1)))
            c0 = s + 256
    return plan


def _proj_kernel(x_ref, g_ref, w_ref, gains_ref, m64_ref, m32_ref, z_ref, zb_ref):
    x = x_ref[...]
    h = x * lax.rsqrt(jnp.mean(x * x, axis=-1, keepdims=True) + EPS) * g_ref[...]
    hb = h.astype(BF16)
    norm_row = {o: r for r, (o, _) in enumerate(_NORMED)}
    for c0, w, gs, slot in _proj_plan():
        z = _dot(hb, w_ref[:, c0:c0 + w])
        if gs:
            m = m64_ref[...] if gs == 64 else m32_ref[...]
            ss = _seg_sum(z * z, m)
            r = norm_row[c0]
            z = z * lax.rsqrt(ss * (1.0 / gs) + EPS) * gains_ref[r:r + 1, :]
        z_ref[:, c0:c0 + w] = z
        if slot >= 0:
            zb_ref[:, slot * 256:(slot + 1) * 256] = z.astype(BF16)


def _proj(x2d, g, w_pad, gains, m64, m32):
    n = x2d.shape[0]
    tm = _row_tile(n, 256)
    return pl.pallas_call(
        _proj_kernel,
        grid=(n // tm,),
        in_specs=[pl.BlockSpec((tm, D_MODEL), lambda i: (i, 0)),
                  pl.BlockSpec((1, D_MODEL), lambda i: (0, 0)),
                  _vmem_full(),
                  pl.BlockSpec((8, 256), lambda i: (0, 0)),
                  pl.BlockSpec((256, 256), lambda i: (0, 0)),
                  pl.BlockSpec((256, 256), lambda i: (0, 0))],
        out_specs=[pl.BlockSpec((tm, P_PAD), lambda i: (i, 0)),
                   pl.BlockSpec((tm, 1536), lambda i: (i, 0))],
        out_shape=[jax.ShapeDtypeStruct((n, P_PAD), F32), jax.ShapeDtypeStruct((n, 1536), BF16)],
        compiler_params=_cparams(("parallel",)),
    )(x2d, g, w_pad, gains, m64, m32)


CMP_K = CMP_BLOCK * 256


def _cmp_kernel(kc_ref, vc_ref, wk_ref, wv_ref, pek_ref, pev_ref, m64_ref, gk_ref,
                kce_ref, kco_ref, vce_ref, vco_ref):
    for src, w_ref, pe_ref, outs, is_k in ((kc_ref, wk_ref, pek_ref, (kce_ref, kco_ref), True),
                                           (vc_ref, wv_ref, pev_ref, (vce_ref, vco_ref), False)):
        for par in range(2):
            rows = src[:, par * CMP_K:(par + 1) * CMP_K] + pe_ref[...]
            acc = _dot(rows.astype(BF16), w_ref[...])
            if is_k:
                ss = _seg_sum(acc * acc, m64_ref[...])
                acc = acc * lax.rsqrt(ss * (1.0 / HD) + EPS) * gk_ref[...]
            outs[par][...] = acc.astype(BF16)


def _cmp(kc2, vc2, wk_bd, wv_bd, pek, pev, m64, gk):
    nr = kc2.shape[0]
    rb = _row_tile(nr, 64)
    out = jax.ShapeDtypeStruct((nr, 256), BF16)
    ospec = pl.BlockSpec((rb, 256), lambda i: (i, 0))
    ispec = pl.BlockSpec((rb, 2 * CMP_K), lambda i: (i, 0))
    par = lambda s: pl.BlockSpec(s, lambda i: (0,) * len(s))
    return pl.pallas_call(
        _cmp_kernel,
        grid=(nr // rb,),
        in_specs=[ispec, ispec, par((CMP_K, 256)), par((CMP_K, 256)), par((1, CMP_K)), par((1, CMP_K)),
                  par((256, 256)), par((1, 256))],
        out_specs=[ospec, ospec, ospec, ospec],
        out_shape=[out, out, out, out],
        compiler_params=_cparams(("parallel",)),
    )(kc2, vc2, wk_bd, wv_bd, pek, pev, m64, gk)


def _softmax_rows(s):
    m = jnp.max(s, axis=-1, keepdims=True)
    e = jnp.exp(s - m)
    return e / jnp.sum(e, axis=-1, keepdims=True)


def _nsa_kernel(q_ref, ng_ref, kce_ref, kco_ref, vce_ref, vco_ref, ks_ref, vs_ref, kw_ref, vw_ref, ex_ref,
                o_ref, *, t_len, tq, kp, band):
    i = pl.program_id(1)
    nsb = t_len // SEL_BLOCK
    q = q_ref[...]
    sig = jax.nn.sigmoid(ng_ref[...])
    lane = _iota((1, 256), 1)
    qpos = i * tq + _iota((tq, 1), 0)
    cur = jnp.right_shift(qpos, 6)
    jb = _iota((1, nsb), 1)
    kpos = _iota((1, t_len), 1)
    dist = (qpos - kpos).astype(F32)
    curf = jnp.where(jnp.right_shift(kpos, 6) == cur, jnp.where(kpos <= qpos, 1.0, 0.0), 0.0)
    de = (qpos - (jb * SEL_BLOCK + (CMP_BLOCK - 1))).astype(F32)
    do = (qpos - (jb * SEL_BLOCK + (SEL_BLOCK - 1))).astype(F32)
    past = jb < cur
    start = jnp.clip(i * tq - WINDOW, 0, t_len - band)
    start = pl.multiple_of(start, tq)
    kposw = start + _iota((1, band), 1)
    dw = qpos - kposw
    okw = jnp.where(dw >= 0, jnp.where(dw < WINDOW, 1.0, 0.0), 0.0) > 0.5
    dwf = dw.astype(F32)
    kwb = kw_ref[pl.ds(start, band), :]
    vwb = vw_ref[pl.ds(start, band), :]
    acc = jnp.zeros((tq, 256), F32)
    for h in range(NSA_H):
        hm = jnp.right_shift(lane, 6) == h
        qh = jnp.where(hm, q, 0.0).astype(BF16)
        sl = NSA_SLOPES[h]
        se = jnp.where(de >= 0, _dot_nt(qh, kce_ref[...]) - sl * de, -jnp.inf)
        so = jnp.where(do >= 0, _dot_nt(qh, kco_ref[...]) - sl * do, -jnp.inf)
        m = jnp.maximum(jnp.max(se, axis=-1, keepdims=True), jnp.max(so, axis=-1, keepdims=True))
        m = jnp.where(m > -jnp.inf, m, 0.0)
        pe = jnp.where(de >= 0, jnp.exp(se - m), 0.0)
        po = jnp.where(do >= 0, jnp.exp(so - m), 0.0)
        den = jnp.sum(pe, axis=-1, keepdims=True) + jnp.sum(po, axis=-1, keepdims=True)
        den = jnp.where(den > 0, den, 1.0)
        pe = pe / den
        po = po / den
        o_c = _dot(pe.astype(BF16), vce_ref[...]) + _dot(po.astype(BF16), vco_ref[...])
        imp = jnp.where(past, pe + po, -1.0)
        rank = jnp.zeros((tq, nsb), F32)
        for j2 in range(nsb):
            col = imp[:, j2:j2 + 1]
            before = jnp.where(jb > j2, 1.0, 0.0)
            rank = rank + jnp.where(col > imp, 1.0, jnp.where(col == imp, before, 0.0))
        member = jnp.where(past, jnp.where(rank < kp, 1.0, 0.0), 0.0)
        allowed = (_dot(member.astype(BF16), ex_ref[...]) + curf) > 0.5
        s = jnp.where(allowed, _dot_nt(qh, ks_ref[...]) - sl * dist, -jnp.inf)
        o_s = _dot(_softmax_rows(s).astype(BF16), vs_ref[...])
        s = jnp.where(okw, _dot_nt(qh, kwb) - sl * dwf, -jnp.inf)
        o_w = _dot(_softmax_rows(s).astype(BF16), vwb)
        y = (sig[:, h:h + 1] * o_c + sig[:, NSA_H + h:NSA_H + h + 1] * o_s
             + sig[:, 2 * NSA_H + h:2 * NSA_H + h + 1] * o_w)
        acc = acc + jnp.where(hm, y, 0.0)
    o_ref[...] = acc


def _nsa_prompt(z, zb, cmp4, expand, bsz, t_len):
    tq = 128
    nt = t_len // tq
    nsb = t_len // SEL_BLOCK
    kp = min(SEL_TOPK - 1, nsb)
    band = min(WINDOW + tq, t_len)
    kv = lambda c: pl.BlockSpec((t_len, 256), lambda b, i: (b, c))
    cs = pl.BlockSpec((nsb, 256), lambda b, i: (b, 0))
    return pl.pallas_call(
        functools.partial(_nsa_kernel, t_len=t_len, tq=tq, kp=kp, band=band),
        grid=(bsz, nt),
        in_specs=[pl.BlockSpec((tq, 256), lambda b, i: (b * nt + i, OFF_Q // 256)),
                  pl.BlockSpec((tq, 128), lambda b, i: (b * nt + i, OFF_NG // 128)),
                  cs, cs, cs, cs, kv(ZB_KS), kv(ZB_VS), kv(ZB_KW), kv(ZB_VW),
                  pl.BlockSpec((nsb, t_len), lambda b, i: (0, 0))],
        out_specs=pl.BlockSpec((tq, 256), lambda b, i: (b * nt + i, 0)),
        out_shape=jax.ShapeDtypeStruct((bsz * t_len, 256), F32),
        compiler_params=_cparams(("parallel", "parallel")),
    )(z, z, *cmp4, zb, zb, zb, zb, expand)


def _diff_kernel(lam_ref, q_ref, k_ref, v_ref, m64_ref, gain_ref, o_ref, *, t_len, tq):
    i = pl.program_id(1)
    lam = lam_ref[0]
    q = q_ref[...]
    lane = _iota((1, 256), 1)
    qpos = i * tq + _iota((tq, 1), 0)
    kpos = _iota((1, t_len), 1)
    causal = kpos <= qpos
    dist = (qpos - kpos).astype(F32)
    acc = jnp.zeros((tq, 256), F32)
    for h in range(DF_H):
        ps = []
        for c in range(2):
            lm = jnp.right_shift(lane, 5) == (2 * h + c)
            qh = jnp.where(lm, q, 0.0).astype(BF16)
            s = jnp.where(causal, _dot_nt(qh, k_ref[...]) - DF_SLOPES[h] * dist, -jnp.inf)
            ps.append(_softmax_rows(s))
        pd = (ps[0] - lam * ps[1]).astype(BF16)
        acc = acc + jnp.where(jnp.right_shift(lane, 6) == h, _dot(pd, v_ref[...]), 0.0)
    ss = _seg_sum(acc * acc, m64_ref[...])
    o_ref[...] = acc * lax.rsqrt(ss * (1.0 / DF_DV) + EPS) * gain_ref[...]


def _diff_prompt(lam, z, zb, m64, gain, bsz, t_len):
    tq = 128
    nt = t_len // tq
    kv = lambda c: pl.BlockSpec((t_len, 256), lambda b, i: (b, c))
    return pl.pallas_call(
        functools.partial(_diff_kernel, t_len=t_len, tq=tq),
        grid=(bsz, nt),
        in_specs=[pl.BlockSpec(memory_space=pltpu.SMEM),
                  pl.BlockSpec((tq, 256), lambda b, i: (b * nt + i, OFF_QD // 256)),
                  kv(ZB_KD), kv(ZB_VD),
                  pl.BlockSpec((256, 256), lambda b, i: (0, 0)),
                  pl.BlockSpec((1, 256), lambda b, i: (0, 0))],
        out_specs=pl.BlockSpec((tq, 256), lambda b, i: (b * nt + i, 0)),
        out_shape=jax.ShapeDtypeStruct((bsz * t_len, 256), F32),
        compiler_params=_cparams(("parallel", "parallel")),
    )(lam, z, zb, zb, m64, gain)


def _shift_rows(x, first_rows, k):
    n = x.shape[0]
    rolled = pltpu.roll(x, k, 0)
    row = _iota((n, 1), 0)
    out = rolled
    for r in range(k):
        out = jnp.where(row == r, first_rows[8 - k + r:8 - k + r + 1, :], out)
    return out


def _seq_kernel(rw_ref, rwp_ref, bg_ref, cg_ref, xc_ref, cgp_ref, xcp_ref, sh0_ref, cv0_ref,
                mu_ref, w0_ref, a0_ref, lw_ref, la_ref, lg_ref, kk_ref, ka_ref, cw_ref, cb_ref, m64_ref,
                r_ref, w_ref, k_ref, v_ref, am_ref, bb_ref, g_ref, ycv_ref, ulast_ref, *, ts):
    i = pl.program_id(1)
    x = rw_ref[...]
    first = jnp.where(i == 0, sh0_ref[0], rwp_ref[...])
    prev = _shift_rows(x, first, 1)
    xm = x + (prev - x) * mu_ref[...]
    r = xm[:, 0:256]
    k = xm[:, 256:512]
    v = xm[:, 512:768]
    lo = xm[:, 768:896]
    wlin = w0_ref[...] + _dot(jnp.tanh(lo).astype(BF16), lw_ref[...])
    w = -jax.nn.softplus(-wlin) - 0.5
    decay = jnp.exp(-jnp.exp(w))
    a = jax.nn.sigmoid(a0_ref[...] + _dot(lo.astype(BF16), la_ref[...]))
    g = _dot(jax.nn.sigmoid(lo).astype(BF16), lg_ref[...])
    kk = k * kk_ref[...]
    nrm = jnp.sqrt(_seg_sum(kk * kk, m64_ref[...]))
    kk = kk / jnp.maximum(nrm, 1e-12)
    r_ref[...] = r
    w_ref[...] = decay
    k_ref[...] = k * (1.0 + (a - 1.0) * ka_ref[...])
    v_ref[...] = v
    am_ref[...] = -kk
    bb_ref[...] = kk * a
    g_ref[...] = g
    u = cg_ref[...] * xc_ref[...]
    ufirst = jnp.where(i == 0, cv0_ref[0], cgp_ref[...] * xcp_ref[...])
    u1 = _shift_rows(u, ufirst, 1)
    u2 = _shift_rows(u, ufirst, 2)
    y = cb_ref[...] + u2 * cw_ref[0:1, :] + u1 * cw_ref[1:2, :] + u * cw_ref[2:3, :]
    ycv_ref[...] = bg_ref[...] * y
    ulast_ref[0] = u[ts - 8:ts, :]


def _seq(z, shift0, conv0, lwp, bsz, t_len):
    ts = _row_tile(t_len, 512)
    nt = t_len // ts
    n = bsz * t_len
    row = lambda w, c: pl.BlockSpec((ts, w), lambda b, i: (b * nt + i, c))
    prv = lambda w, c: pl.BlockSpec((8, w), lambda b, i: (jnp.maximum((b * nt + i) * (ts // 8) - 1, 0), c))
    par = lambda s: pl.BlockSpec(s, lambda b, i: (0,) * len(s))
    o256 = pl.BlockSpec((ts, 256), lambda b, i: (b * nt + i, 0))
    sd = jax.ShapeDtypeStruct((n, 256), F32)
    return pl.pallas_call(
        functools.partial(_seq_kernel, ts=ts),
        grid=(bsz, nt),
        in_specs=[row(RW_P, OFF_RW // RW_P), prv(RW_P, OFF_RW // RW_P),
                  row(256, OFF_BG // 256), row(256, OFF_CG // 256), row(256, OFF_XC // 256),
                  prv(256, OFF_CG // 256), prv(256, OFF_XC // 256),
                  pl.BlockSpec((1, 8, RW_P), lambda b, i: (b, 0, 0)),
                  pl.BlockSpec((1, 8, 256), lambda b, i: (b, 0, 0)),
                  par((1, RW_P)), par((1, 256)), par((1, 256)), par((128, 256)), par((128, 256)), par((128, 256)),
                  par((1, 256)), par((1, 256)), par((8, 256)), par((1, 256)), par((256, 256))],
        out_specs=[o256] * 8 + [pl.BlockSpec((1, 8, 256), lambda b, i: (b, 0, 0))],
        out_shape=[sd] * 8 + [jax.ShapeDtypeStruct((bsz, 8, 256), F32)],
        compiler_params=_cparams(("parallel", "arbitrary")),
    )(z, z, z, z, z, z, z, shift0, conv0, lwp['mu'], lwp['w0'], lwp['a0'], lwp['lw'], lwp['la'], lwp['lg'],
      lwp['kk'], lwp['ka'], lwp['cw'], lwp['cb'], lwp['m64'])


def _scan_kernel(s0_ref, am_ref, w_ref, bb_ref, k_ref, r_ref, v_ref, y_ref, s_ref, *, tc, ng, ih):
    @pl.when(pl.program_id(0) == 0)
    def _():
        s_ref[...] = s0_ref[...]

    def step(t, carry):
        def group(g, c2):
            h = g // ih
            hs = pl.multiple_of(h * RW_N, RW_N)
            st = s_ref[g]
            a_c = am_ref[t, pl.ds(hs, RW_N), :]
            sa = jnp.sum(st * a_c, axis=0, keepdims=True)
            vrow = v_ref[t, pl.ds(g, 1), :]
            st = (st * w_ref[t, pl.ds(hs, RW_N), :] + sa * bb_ref[t, pl.ds(hs, RW_N), :]
                  + vrow * k_ref[t, pl.ds(hs, RW_N), :])
            s_ref[g] = st
            y_ref[t, pl.ds(g, 1), :] = jnp.sum(st * r_ref[t, pl.ds(hs, RW_N), :], axis=0, keepdims=True)
            return c2

        return lax.fori_loop(0, ng, group, carry, unroll=4)

    lax.fori_loop(0, tc, step, 0)


def _scan(s0, am, w, bb, k, r, v, t_len, ih):
    ng = RW_H * ih
    tc = _row_tile(t_len, 16)
    col = pl.BlockSpec((tc, 256, LANES), lambda i: (i, 0, 0))
    rowv = pl.BlockSpec((tc, ng, LANES), lambda i: (i, 0, 0))
    st = pl.BlockSpec((ng, RW_N, LANES), lambda i: (0, 0, 0))
    return pl.pallas_call(
        functools.partial(_scan_kernel, tc=tc, ng=ng, ih=ih),
        grid=(t_len // tc,),
        in_specs=[st, col, col, col, col, col, rowv],
        out_specs=[rowv, st],
        out_shape=[jax.ShapeDtypeStruct((t_len, ng, LANES), F32), jax.ShapeDtypeStruct((ng, RW_N, LANES), F32)],
        compiler_params=_cparams(("arbitrary",)),
    )(s0, am, w, bb, k, r, v)


def _to_cols(x, bsz, t_len, il):
    x = x.reshape(bsz, t_len, 256).transpose(1, 2, 0)
    return jnp.tile(x, (1, 1, il))


def _to_rows(x, bsz, t_len, il, ih):
    x = x.reshape(bsz, t_len, RW_H, ih, il).transpose(1, 2, 3, 4, 0)
    return x.reshape(t_len, RW_H * ih, il * bsz)


def _from_rows(y, bsz, t_len, il, ih):
    y = y.reshape(t_len, RW_H, ih, il, bsz).transpose(4, 0, 1, 2, 3)
    return y.reshape(bsz * t_len, 256)


def _state_in(s, bsz, il, ih):
    s = s.reshape(bsz, RW_H, ih, il, RW_N).transpose(1, 2, 4, 3, 0)
    return s.reshape(RW_H * ih, RW_N, il * bsz)


def _state_out(s, bsz, il, ih):
    s = s.reshape(RW_H, ih, RW_N, il, bsz).transpose(4, 0, 1, 3, 2)
    return s.reshape(bsz, RW_H, RW_N, RW_N)


def _rwkv_scan(seq_out, s0, bsz, t_len):
    r, w, k, v, am, bb = seq_out
    il = LANES // bsz
    ih = RW_N // il
    cols = [_to_cols(a, bsz, t_len, il) for a in (am, w, bb, k, r)]
    y, s_fin = _scan(_state_in(s0, bsz, il, ih), *cols, _to_rows(v, bsz, t_len, il, ih), t_len, ih)
    return _from_rows(y, bsz, t_len, il, ih), _state_out(s_fin, bsz, il, ih)


def _merge_kernel(x_ref, gate_ref, ynsa_ref, yrw_ref, r_ref, k_ref, v_ref, g_ref, ycv_ref, ydf_ref,
                  wbr_ref, wo_ref, lnw_ref, lnb_ref, rk_ref, m64_ref, nx_ref, wxq_ref, qg_ref,
                  x1_ref, qx_ref):
    m64 = m64_ref[...]
    y = yrw_ref[...]
    mu = _seg_sum(y, m64) * (1.0 / RW_N)
    d = y - mu
    var = _seg_sum(d * d, m64) * (1.0 / RW_N)
    yn = d * lax.rsqrt(var + GN_EPS) * lnw_ref[...] + lnb_ref[...]
    v = v_ref[...]
    bonus = _seg_sum(r_ref[...] * k_ref[...] * rk_ref[...], m64) * v
    y_rw = (yn + bonus) * g_ref[...]
    ys = (ynsa_ref[...], y_rw, ycv_ref[...], ydf_ref[...])
    zsum = None
    for b in range(N_BRANCH):
        t = jax.nn.sigmoid(gate_ref[:, b * D_MODEL:(b + 1) * D_MODEL]) * _dot(ys[b].astype(BF16), wbr_ref[b])
        zsum = t if zsum is None else zsum + t
    x1 = x_ref[...] + _dot(zsum.astype(BF16), wo_ref[...])
    x1_ref[...] = x1
    hx = x1 * lax.rsqrt(jnp.mean(x1 * x1, axis=-1, keepdims=True) + EPS) * nx_ref[...]
    qx = _dot(hx.astype(BF16), wxq_ref[...])
    ss = _seg_sum(qx * qx, m64)
    qx_ref[...] = qx * lax.rsqrt(ss * (1.0 / HD) + EPS) * qg_ref[...]


def _merge(x2d, z, ynsa, yrw, r, k, v, g, ycv, ydf, mw):
    n = x2d.shape[0]
    tm = _row_tile(n, 256)
    row = lambda w, c=0: pl.BlockSpec((tm, w), lambda i: (i, c))
    par = lambda s: pl.BlockSpec(s, lambda i: (0,) * len(s))
    return pl.pallas_call(
        _merge_kernel,
        grid=(n // tm,),
        in_specs=[row(D_MODEL), row(4 * D_MODEL, OFF_GATE)] + [row(256)] * 8
                 + [par((4, 256, D_MODEL)), par((D_MODEL, D_MODEL)), par((1, 256)), par((1, 256)), par((1, 256)),
                    par((256, 256)), par((1, D_MODEL)), par((D_MODEL, 256)), par((1, 256))],
        out_specs=[row(D_MODEL), row(256)],
        out_shape=[jax.ShapeDtypeStruct((n, D_MODEL), F32), jax.ShapeDtypeStruct((n, 256), F32)],
        compiler_params=_cparams(("parallel",)),
    )(x2d, z, ynsa, yrw, r, k, v, g, ycv, ydf, mw['wbr'], mw['wo'], mw['lnw'], mw['lnb'], mw['rk'], mw['m64'],
      mw['nx'], mw['wxq'], mw['qg'])


def _memkv_kernel(x_ref, g_ref, w_ref, m64_ref, kg_ref, k_ref, v_ref):
    x = x_ref[...]
    h = (x * lax.rsqrt(jnp.mean(x * x, axis=-1, keepdims=True) + EPS) * g_ref[...]).astype(BF16)
    k = _dot(h, w_ref[:, 0:256])
    ss = _seg_sum(k * k, m64_ref[...])
    k_ref[...] = k * lax.rsqrt(ss * (1.0 / HD) + EPS) * kg_ref[...]
    v_ref[...] = _dot(h, w_ref[:, 256:512])


def _memkv(mem2d, g, wkv, m64, kg):
    n = mem2d.shape[0]
    tm = _row_tile(n, 256)
    par = lambda s: pl.BlockSpec(s, lambda i: (0,) * len(s))
    row = lambda w: pl.BlockSpec((tm, w), lambda i: (i, 0))
    sd = jax.ShapeDtypeStruct((n, 256), F32)
    return pl.pallas_call(
        _memkv_kernel,
        grid=(n // tm,),
        in_specs=[row(D_MODEL), par((1, D_MODEL)), par((D_MODEL, 512)), par((256, 256)), par((1, 256))],
        out_specs=[row(256), row(256)],
        out_shape=[sd, sd],
        compiler_params=_cparams(("parallel",)),
    )(mem2d, g, wkv, m64, kg)


def _xattn_kernel(qx_ref, mk_ref, mv_ref, x1_ref, wxo_ref, nf_ref, x2_ref, hf_ref):
    q = qx_ref[...]
    lane = _iota((1, 256), 1)
    mk = mk_ref[0]
    mv = mv_ref[0]
    acc = jnp.zeros(q.shape, F32)
    for h in range(MX_H):
        hm = jnp.right_shift(lane, 6) == h
        s = _dot_nt(jnp.where(hm, q, 0.0).astype(BF16), mk)
        acc = acc + jnp.where(hm, _dot(_softmax_rows(s).astype(BF16), mv), 0.0)
    x2 = x1_ref[...] + _dot(acc.astype(BF16), wxo_ref[...])
    x2_ref[...] = x2
    hf_ref[...] = x2 * lax.rsqrt(jnp.mean(x2 * x2, axis=-1, keepdims=True) + EPS) * nf_ref[...]


def _xattn(qx, mk, mv, x1, wxo, nf, bsz, t_len):
    tq = _row_tile(t_len, 256)
    nt = t_len // tq
    mlen = mk.shape[1]
    row = lambda w: pl.BlockSpec((tq, w), lambda b, i: (b * nt + i, 0))
    mem = pl.BlockSpec((1, mlen, 256), lambda b, i: (b, 0, 0))
    sd = jax.ShapeDtypeStruct((bsz * t_len, D_MODEL), F32)
    return pl.pallas_call(
        _xattn_kernel,
        grid=(bsz, nt),
        in_specs=[row(256), mem, mem, row(D_MODEL),
                  pl.BlockSpec((256, D_MODEL), lambda b, i: (0, 0)),
                  pl.BlockSpec((1, D_MODEL), lambda b, i: (0, 0))],
        out_specs=[row(D_MODEL), row(D_MODEL)],
        out_shape=[sd, sd],
        compiler_params=_cparams(("parallel", "parallel")),
    )(qx, mk, mv, x1, wxo, nf)


PEER_TILE = 128


def _topk_cols(s, ids, n_take, payload=None):
    big = s.shape[0]
    vals, picks = [], []
    for _ in range(n_take):
        m = jnp.max(s, axis=0, keepdims=True)
        ix = jnp.min(jnp.where(s == m, ids, big), axis=0, keepdims=True)
        hit = ids == ix
        vals.append(m)
        picks.append(ix if payload is None else jnp.max(jnp.where(hit, payload, -1), axis=0, keepdims=True))
        s = jnp.where(hit, -jnp.inf, s)
    return jnp.concatenate(vals, axis=0), jnp.concatenate(picks, axis=0)


def _peer_score_kernel(hf_ref, wq_ref, kbd_ref, e_ref, g_ref):
    q = _dot(hf_ref[...].astype(BF16), wq_ref[...])
    st = _dot_nt(kbd_ref[...], q.astype(BF16))
    kid = _iota((N_KEYS, PEER_TILE), 0)
    cid = _iota((PK_TOPK * PK_TOPK, PEER_TILE), 0)
    for h in range(PK_H):
        sv0, si0 = _topk_cols(st[(2 * h) * N_KEYS:(2 * h + 1) * N_KEYS, :], kid, PK_TOPK)
        sv1, si1 = _topk_cols(st[(2 * h + 1) * N_KEYS:(2 * h + 2) * N_KEYS, :], kid, PK_TOPK)
        cand = jnp.concatenate([sv0[a:a + 1, :] + sv1 for a in range(PK_TOPK)], axis=0)
        ecand = jnp.concatenate([si0[a:a + 1, :] * N_KEYS + si1 for a in range(PK_TOPK)], axis=0)
        cv, ev = _topk_cols(cand, cid, PK_TOPK, payload=ecand)
        ex = jnp.exp(cv - cv[0:1, :])
        g_ref[h * PK_TOPK:(h + 1) * PK_TOPK, :] = ex / jnp.sum(ex, axis=0, keepdims=True)
        e_ref[h * PK_TOPK:(h + 1) * PK_TOPK, :] = ev


def _peer_score(hf, wq, kbd):
    n = hf.shape[0]
    nt = n // PEER_TILE
    nsel = PK_H * PK_TOPK
    out = pl.BlockSpec((nsel, PEER_TILE), lambda i: (i, 0))
    return pl.pallas_call(
        _peer_score_kernel,
        grid=(nt,),
        in_specs=[pl.BlockSpec((PEER_TILE, D_MODEL), lambda i: (i, 0)),
                  pl.BlockSpec((D_MODEL, PK_H * PK_DK), lambda i: (0, 0)),
                  pl.BlockSpec((PK_H * 2 * N_KEYS, PK_H * PK_DK), lambda i: (0, 0))],
        out_specs=[out, out],
        out_shape=[jax.ShapeDtypeStruct((nt * nsel, PEER_TILE), I32),
                   jax.ShapeDtypeStruct((nt * nsel, PEER_TILE), F32)],
        compiler_params=_cparams(("parallel",)),
    )(hf, wq, kbd)


def _unpack_pair(wd):
    lo = pltpu.bitcast(jnp.left_shift(wd, 16), F32)
    hi = pltpu.bitcast(jnp.bitwise_and(wd, jnp.uint32(0xFFFF0000)), F32)
    return lo, hi


PEER_GT = 64
NSEL = PK_H * PK_TOPK


def _peer_u_kernel(tix_ref, par_ref, x_ref, tab_ref, act_ref, a_ref):
    sub = _iota((8, LANES), 0)

    def token(n, carry):
        x8 = x_ref[n]
        xr = pltpu.roll(x8, 4, 0)
        xl = jnp.where(sub < 4, x8, xr)
        xh = jnp.where(sub < 4, xr, x8)
        base = n * NSEL
        for j in range(NSEL):
            lo, hi = _unpack_pair(tab_ref[tix_ref[0, 0, base + j]])
            a_ref[j * 8:(j + 1) * 8, :] = lo * xl + hi * xh
        halves = []
        for s0 in (0, 4):
            acc = a_ref[pl.ds(s0, NSEL, stride=8), :]
            for s in range(s0 + 1, s0 + 4):
                acc = acc + a_ref[pl.ds(s, NSEL, stride=8), :]
            halves.append(jnp.sum(acc.T, axis=0, keepdims=True))
        act_ref[pl.ds(n, 1), :] = jnp.where(par_ref[pl.ds(n, 1), :] == 0, halves[0], halves[1])
        return carry

    lax.fori_loop(0, PEER_GT, token, 0)


def _peer_u(tix3, par, hf8, tab):
    nt = tix3.shape[0]
    return pl.pallas_call(
        _peer_u_kernel,
        grid=(nt,),
        in_specs=[pl.BlockSpec((1, 1, PEER_GT * NSEL), lambda i: (i, 0, 0), memory_space=pltpu.SMEM),
                  pl.BlockSpec((PEER_GT, NSEL), lambda i: (i, 0)),
                  pl.BlockSpec((PEER_GT, 8, LANES), lambda i: (i, 0, 0)),
                  _vmem_full()],
        out_specs=pl.BlockSpec((PEER_GT, NSEL), lambda i: (i, 0)),
        out_shape=jax.ShapeDtypeStruct((nt * PEER_GT, NSEL), F32),
        scratch_shapes=[pltpu.VMEM((NSEL * 8, LANES), F32)],
        compiler_params=_cparams(("parallel",)),
    )(tix3, par, hf8, tab)


def _peer_v_kernel(tix_ref, c8_ref, x_ref, tab_ref, o_ref):
    low = _iota((8, LANES), 0) < 4
    n_acc = 4

    def token(n, carry):
        base = n * NSEL
        c8 = c8_ref[n]
        alo = [jnp.zeros((8, LANES), F32) for _ in range(n_acc)]
        ahi = [jnp.zeros((8, LANES), F32) for _ in range(n_acc)]
        for j in range(NSEL):
            lo, hi = _unpack_pair(tab_ref[tix_ref[0, 0, base + j]])
            cm = jnp.broadcast_to(c8[:, j:j + 1], (8, LANES))
            alo[j % n_acc] = alo[j % n_acc] + cm * lo
            ahi[j % n_acc] = ahi[j % n_acc] + cm * hi
        tl = (alo[0] + alo[1]) + (alo[2] + alo[3])
        th = (ahi[0] + ahi[1]) + (ahi[2] + ahi[3])
        tl = tl + pltpu.roll(tl, 4, 0)
        th = th + pltpu.roll(th, 4, 0)
        o_ref[n] = x_ref[n] + jnp.where(low, tl, th)
        return carry

    lax.fori_loop(0, PEER_GT, token, 0)


def _peer_v(tix3, c8, x8, tab):
    nt = tix3.shape[0]
    sm = pl.BlockSpec((1, 1, PEER_GT * NSEL), lambda i: (i, 0, 0), memory_space=pltpu.SMEM)
    xs = pl.BlockSpec((PEER_GT, 8, LANES), lambda i: (i, 0, 0))
    return pl.pallas_call(
        _peer_v_kernel,
        grid=(nt,),
        in_specs=[sm, pl.BlockSpec((PEER_GT, 8, NSEL), lambda i: (i, 0, 0)), xs, _vmem_full()],
        out_specs=xs,
        out_shape=jax.ShapeDtypeStruct(x8.shape, F32),
        compiler_params=_cparams(("parallel",)),
    )(tix3, c8, x8, tab)


def _pack_table(t):
    e = t.shape[0]
    b = lax.bitcast_convert_type(t.astype(BF16), jnp.uint16).astype(jnp.uint32)
    w = b[:, :512] | (b[:, 512:] << 16)
    return w.reshape(e // 2, 8, LANES)


def _peer(hf, x2, pw):
    n = hf.shape[0]
    nt = n // PEER_TILE
    nsel = PK_H * PK_TOPK
    e_t, g_t = _peer_score(hf, pw['wq'], pw['kbd'])
    tok_major = lambda a: a.reshape(nt, nsel, PEER_TILE).transpose(0, 2, 1).reshape(n, nsel)
    e_tm = tok_major(e_t)
    par = jnp.bitwise_and(e_tm, 1)
    smem = lambda a: a.reshape(n // PEER_GT, 1, PEER_GT * nsel)
    tix3 = smem(jnp.right_shift(e_tm, 1))
    act = _peer_u(tix3, par, hf.reshape(n, 8, LANES), pw['u'])
    c = tok_major(g_t) * jax.nn.gelu(act, approximate=False)
    ce = jnp.where(par == 0, c, 0.0)[:, None, :]
    co = jnp.where(par == 1, c, 0.0)[:, None, :]
    c8 = jnp.concatenate([jnp.broadcast_to(ce, (n, 4, nsel)), jnp.broadcast_to(co, (n, 4, nsel))], axis=1)
    out8 = _peer_v(tix3, c8, x2.reshape(n, 8, LANES), pw['v'])
    return out8.reshape(n, D_MODEL)


def _masked_softmax(s, mask):
    s = jnp.where(mask, s, -jnp.inf)
    m = jnp.max(s, axis=-1, keepdims=True)
    m = jnp.where(jnp.isfinite(m), m, 0.0)
    e = jnp.where(mask, jnp.exp(s - m), 0.0)
    den = jnp.sum(e, axis=-1, keepdims=True)
    return e / jnp.where(den > 0, den, 1.0)


def _gather_pages(pool, page_table):
    g = pool[page_table]
    return g.reshape((g.shape[0], g.shape[1] * g.shape[2]) + g.shape[3:])


def _rms(x, g):
    return x * lax.rsqrt(jnp.mean(x * x, axis=-1, keepdims=True) + EPS) * g


def _compress(rows, w_c, pe):
    b, l, h, d = rows.shape
    blk = rows.reshape(b, l // CMP_BLOCK, CMP_BLOCK, h, d) + pe[:, None, :]
    return jnp.einsum('bjihd,ide->bjhe', blk, w_c)


def _sample_attn(zs, c, page_table, lw, lam, lam_init, bsz, t_len, past_len):
    col = lambda off, w=256: zs[:, off:off + w].reshape(bsz, t_len, -1)
    hd = lambda a: a.reshape(bsz, t_len, NSA_H, HD)
    q, ks, kw = hd(col(OFF_Q)), hd(col(OFF_KS)), hd(col(OFF_KW))
    kc_r, vc_r, vs, vw = hd(col(OFF_KC)), hd(col(OFF_VC)), hd(col(OFF_VS)), hd(col(OFF_VW))
    gates = jax.nn.sigmoid(col(OFF_NG, 128)[..., :3 * NSA_H].reshape(bsz, t_len, 3, NSA_H, 1))
    qpos = past_len + jnp.arange(t_len)
    ns = jnp.asarray(NSA_SLOPES, F32)
    n_new = (past_len + t_len) // CMP_BLOCK - past_len // CMP_BLOCK

    def cmp_rows(pool, new, w_c, pe):
        zc = _compress(_gather_pages(pool, page_table), w_c, pe)
        if n_new > 0:
            zc = jnp.concatenate([zc, _compress(new[:, :n_new * CMP_BLOCK], w_c, pe)], axis=1)
        return zc

    kc = _rms(cmp_rows(c['cmp_k'], kc_r, lw['nsa_ck_w'], lw['nsa_ck_pe']), lw['nsa_kn'][0])
    vc = cmp_rows(c['cmp_v'], vc_r, lw['nsa_cv_w'], lw['nsa_cv_pe'])
    nc = kc.shape[1]
    end = jnp.arange(nc) * CMP_BLOCK + (CMP_BLOCK - 1)
    dist = (qpos[:, None] - end[None, :]).astype(F32)
    s = jnp.einsum('bthd,bjhd->bhtj', q, kc).astype(F32) - ns[:, None, None] * dist
    p_c = _masked_softmax(s, dist >= 0)
    o_c = jnp.einsum('bhtj,bjhd->bthd', p_c, vc)
    nsb = past_len // SEL_BLOCK
    kp = min(SEL_TOPK - 1, nsb)
    imp = p_c[..., :nsb * 2].reshape(bsz, NSA_H, t_len, nsb, 2).sum(-1)
    cur = qpos // SEL_BLOCK
    imp = jnp.where(jnp.arange(nsb)[None, :] < cur[:, None], imp, -1.0)
    _, idx = lax.top_k(imp, kp)
    sub_n = PAGE_SIZE // SEL_BLOCK
    bi = jnp.arange(bsz)[:, None, None, None]
    hi = jnp.arange(NSA_H)[None, :, None, None]
    phys = page_table[bi, idx // sub_n]
    sub = idx % sub_n

    def gather(pool):
        pr = pool.reshape((pool.shape[0], sub_n, SEL_BLOCK) + pool.shape[2:])
        return pr[phys, sub, :, hi, :]

    qh = q.transpose(0, 2, 1, 3)
    kg, vg = gather(c['sel_k']), gather(c['sel_v'])
    kcur, vcur = ks.transpose(0, 2, 1, 3), vs.transpose(0, 2, 1, 3)
    sl = ns.reshape(1, NSA_H, 1, 1)
    kpos = idx[..., None] * SEL_BLOCK + jnp.arange(SEL_BLOCK)
    s1 = jnp.einsum('bhtd,bhtksd->bhtks', qh, kg).astype(F32)
    s1 = s1 - sl[..., None] * (qpos[:, None, None] - kpos).astype(F32)
    m1 = jnp.broadcast_to((idx < cur[:, None])[..., None], s1.shape)
    s2 = jnp.einsum('bhtd,bhsd->bhts', qh, kcur).astype(F32)
    s2 = s2 - sl * (qpos[:, None] - qpos[None, :]).astype(F32)
    m2 = (qpos[None, :] <= qpos[:, None]) & (qpos[None, :] // SEL_BLOCK == cur[:, None])
    m2 = jnp.broadcast_to(m2, s2.shape)
    k_s = kp * SEL_BLOCK
    sa = jnp.concatenate([s1.reshape(bsz, NSA_H, t_len, k_s), s2], axis=-1)
    ma = jnp.concatenate([m1.reshape(bsz, NSA_H, t_len, k_s), m2], axis=-1)
    p = _masked_softmax(sa, ma)
    o_s = (jnp.einsum('bhtks,bhtksd->bhtd', p[..., :k_s].reshape(bsz, NSA_H, t_len, kp, SEL_BLOCK), vg)
           + jnp.einsum('bhts,bhsd->bhtd', p[..., k_s:], vcur)).transpose(0, 2, 1, 3)
    buf_k, buf_v = c['win_k'], c['win_v']
    wb = buf_k.shape[1]
    kposw = jnp.concatenate([past_len - wb + jnp.arange(wb), qpos])
    dw = qpos[:, None] - kposw[None, :]
    s = jnp.concatenate([jnp.einsum('bthd,bshd->bhts', q, buf_k),
                         jnp.einsum('bthd,bshd->bhts', q, kw)], axis=-1).astype(F32)
    s = s - ns.reshape(1, NSA_H, 1, 1) * dw.astype(F32)
    p = _masked_softmax(s, (dw >= 0) & (dw < WINDOW))
    o_w = jnp.einsum('bhts,bshd->bthd', p[..., :wb], buf_v) + jnp.einsum('bhts,bshd->bthd', p[..., wb:], vw)
    y_nsa = (gates[:, :, 0] * o_c + gates[:, :, 1] * o_s + gates[:, :, 2] * o_w).reshape(bsz * t_len, MIX_W)
    qd = col(OFF_QD).reshape(bsz, t_len, DF_H, 2, DF_DQK)
    kd = col(OFF_KD).reshape(bsz, t_len, DF_H, 2, DF_DQK)
    vd = col(OFF_VD).reshape(bsz, t_len, DF_H, DF_DV)
    kpg = _gather_pages(c['diff_k'], page_table)
    plen = kpg.shape[1]
    kpg = kpg.reshape(bsz, plen, DF_H, 2, DF_DQK)
    vpg = _gather_pages(c['diff_v'], page_table)
    kposd = jnp.concatenate([jnp.arange(plen), qpos])
    dd = (qpos[:, None] - kposd[None, :]).astype(F32)
    s = jnp.concatenate([jnp.einsum('bqhcd,bshcd->bhcqs', qd, kpg),
                         jnp.einsum('bqhcd,bshcd->bhcqs', qd, kd)], axis=-1).astype(F32)
    s = s - jnp.asarray(DF_SLOPES, F32).reshape(1, DF_H, 1, 1, 1) * dd
    p = _masked_softmax(s, dd >= 0)
    pd = p[:, :, 0] - lam * p[:, :, 1]
    o_d = jnp.einsum('bhqs,bshd->bqhd', pd[..., :plen], vpg) + jnp.einsum('bhqs,bshd->bqhd', pd[..., plen:], vd)
    y_df = (_rms(o_d, lw['df_subln']) * (1.0 - lam_init)).reshape(bsz * t_len, MIX_W)
    return y_nsa, y_df


def _block_ones(gs):
    i = np.arange(256)
    return jnp.asarray((i[:, None] // gs) == (i[None, :] // gs), BF16)


def _tile_heads(v, reps):
    return jnp.tile(v.astype(F32), reps).reshape(1, -1)


def _prep_layer(lw, l):
    m64, m32 = _block_ones(64), _block_ones(32)
    w_in = lw['w_in']
    src = np.cumsum([0, 256, 256, 256, 256, 256, 256, 256, 12, RW_P, 256, 256, 256, 256, 256, 256, 4096])
    names = ['q', 'kc', 'vc', 'ks', 'vs', 'kw', 'vw', 'ng', 'rw', 'bg', 'cg', 'xc', 'qd', 'kd', 'vd', 'gate']
    dst = dict(q=OFF_Q, kc=OFF_KC, vc=OFF_VC, ks=OFF_KS, vs=OFF_VS, kw=OFF_KW, vw=OFF_VW, ng=OFF_NG, rw=OFF_RW,
               bg=OFF_BG, cg=OFF_CG, xc=OFF_XC, qd=OFF_QD, kd=OFF_KD, vd=OFF_VD, gate=OFF_GATE)
    order = sorted(names, key=lambda nm: dst[nm])
    pieces = []
    pos = 0
    for nm in order:
        k = names.index(nm)
        assert dst[nm] == pos, (nm, dst[nm], pos)
        piece = w_in[:, int(src[k]):int(src[k + 1])]
        if nm == 'ng':
            piece = jnp.pad(piece, ((0, 0), (0, 128 - 3 * NSA_H)))
        pieces.append(piece)
        pos += piece.shape[1]
    assert pos == P_PAD
    w_pad = jnp.concatenate(pieces, axis=1).astype(BF16)
    gains = jnp.concatenate([
        _tile_heads(lw['nsa_qn'], 4) * (HD ** -0.5), _tile_heads(lw['nsa_kn'][1], 4), _tile_heads(lw['nsa_kn'][2], 4),
        _tile_heads(lw['df_qn'], 8) * (DF_DQK ** -0.5), _tile_heads(lw['df_kn'], 8),
        jnp.zeros((3, 256), F32)], axis=0)
    eye4 = jnp.eye(NSA_H, dtype=F32)
    bd = lambda w: jnp.einsum('hg,ide->ihdge', eye4, w).reshape(CMP_K, 256).astype(BF16)
    lam_init = 0.8 - 0.6 * math.exp(-0.3 * l)
    lam = (jnp.exp(jnp.sum(lw['df_lq1'] * lw['df_lk1'])) - jnp.exp(jnp.sum(lw['df_lq2'] * lw['df_lk2'])) + lam_init)
    lora = lambda w, a, b: jnp.zeros((128, 256), F32).at[a:b].set(w).astype(BF16)
    seqw = dict(mu=lw['rw_mu'].reshape(1, RW_P), w0=lw['rw_w0'].reshape(1, 256), a0=lw['rw_a0'].reshape(1, 256),
                lw=lora(lw['rw_w2'], 0, 32), la=lora(lw['rw_a2'], 32, 64), lg=lora(lw['rw_g2'], 64, 128),
                kk=lw['rw_kk'].reshape(1, 256), ka=lw['rw_ka'].reshape(1, 256),
                cw=jnp.pad(lw['conv_w'], ((0, 8 - CONV_K), (0, 0))), cb=lw['conv_b'].reshape(1, 256), m64=m64)
    mw = dict(wbr=lw['w_br'].astype(BF16), wo=lw['w_o'].astype(BF16), lnw=lw['rw_lnw'].reshape(1, 256),
              lnb=lw['rw_lnb'].reshape(1, 256), rk=lw['rw_rk'].reshape(1, 256), m64=m64,
              nx=lw['norm_x'].reshape(1, D_MODEL), wxq=lw['w_xq'].astype(BF16),
              qg=_tile_heads(lw['x_qn'], 4) * (HD ** -0.5))
    keys = lw['pk_keys'].reshape(PK_H * 2, N_KEYS, PK_DK // 2)
    eye16 = jnp.eye(PK_H * 2, dtype=F32)
    kbd = jnp.einsum('gf,gkd->gkfd', eye16, keys).reshape(PK_H * 2 * N_KEYS, PK_H * PK_DK).astype(BF16)
    pw = dict(wq=lw['pk_wq'].astype(BF16), kbd=kbd, u=_pack_table(lw['pk_u']), v=_pack_table(lw['pk_v']))
    return dict(
        g_mix=lw['norm_mix'].reshape(1, D_MODEL), w_pad=w_pad, gains=gains, m64=m64, m32=m32,
        wk_bd=bd(lw['nsa_ck_w']), wv_bd=bd(lw['nsa_cv_w']),
        pek=jnp.tile(lw['nsa_ck_pe'], (1, NSA_H)).reshape(1, CMP_K),
        pev=jnp.tile(lw['nsa_cv_pe'], (1, NSA_H)).reshape(1, CMP_K),
        gk0=_tile_heads(lw['nsa_kn'][0], 4), lam=lam.reshape(1).astype(F32), lam_init=lam_init,
        df_gain=_tile_heads(lw['df_subln'], 4) * (1.0 - lam_init), seqw=seqw, mw=mw, pw=pw,
        g_mem=lw['norm_mem'].reshape(1, D_MODEL),
        wkv=jnp.concatenate([lw['w_xk'], lw['w_xv']], axis=1).astype(BF16),
        kg=_tile_heads(lw['x_kn'], 4), wxo=lw['w_xo'].astype(BF16), nf=lw['norm_ffn'].reshape(1, D_MODEL))


def _pad_rows(x, mult):
    n = x.shape[0]
    p = (-n) % mult
    return x if p == 0 else jnp.pad(x, ((0, p),) + ((0, 0),) * (x.ndim - 1))


def _tail(x2d, z, ys, rw, mk, mv, pl_, bsz, t_len):
    y_nsa, y_scan, y_cv, y_df = ys
    r, k, v, g = rw
    x1, qx = _merge(x2d, z, y_nsa, y_scan, r, k, v, g, y_cv, y_df, pl_['mw'])
    x2, hf = _xattn(qx, mk, mv, x1, pl_['wxo'], pl_['nf'], bsz, t_len)
    n = x2.shape[0]
    out = _peer(_pad_rows(hf, PEER_TILE), _pad_rows(x2, PEER_TILE), pl_['pw'])
    return out[:n]


def _prompt_layer(x2d, mem2d, pl_, bsz, t_len):
    n = bsz * t_len
    z, zb = _proj(x2d, pl_['g_mix'], pl_['w_pad'], pl_['gains'], pl_['m64'], pl_['m32'])
    kc_raw, vc_raw = z[:, OFF_KC:OFF_KC + 256], z[:, OFF_VC:OFF_VC + 256]
    cmp4 = _cmp(kc_raw.reshape(n // SEL_BLOCK, 2 * CMP_K), vc_raw.reshape(n // SEL_BLOCK, 2 * CMP_K),
                pl_['wk_bd'], pl_['wv_bd'], pl_['pek'], pl_['pev'], pl_['m64'], pl_['gk0'])
    nsb = t_len // SEL_BLOCK
    expand = jnp.asarray(np.arange(nsb)[:, None] == (np.arange(t_len)[None, :] // SEL_BLOCK), BF16)
    y_nsa = _nsa_prompt(z, zb, cmp4, expand, bsz, t_len)
    y_df = _diff_prompt(pl_['lam'], z, zb, pl_['m64'], pl_['df_gain'], bsz, t_len)
    seq = _seq(z, jnp.zeros((bsz, 8, RW_P), F32), jnp.zeros((bsz, 8, 256), F32), pl_['seqw'], bsz, t_len)
    r, w, k, v, am, bb, g, y_cv, ulast = seq
    y_scan, s_fin = _rwkv_scan((r, w, k, v, am, bb), jnp.zeros((bsz, RW_H, RW_N, RW_N), F32), bsz, t_len)
    mk, mv = _memkv(mem2d, pl_['g_mem'], pl_['wkv'], pl_['m64'], pl_['kg'])
    mlen = mem2d.shape[0] // bsz
    mk3, mv3 = mk.reshape(bsz, mlen, 256), mv.reshape(bsz, mlen, 256)
    x_out = _tail(x2d, z, (y_nsa, y_scan, y_cv, y_df), (r, k, v, g), mk3.astype(BF16), mv3.astype(BF16),
                  pl_, bsz, t_len)
    st = lambda off: z[:, off:off + 256].reshape(bsz, t_len, NSA_H, HD)
    wb = min(WINDOW, t_len)
    shift = z[:, OFF_RW:OFF_RW + RW_P].reshape(bsz, t_len, RW_P)[:, -1]
    states = (st(OFF_KC), st(OFF_VC), st(OFF_KS), st(OFF_VS), st(OFF_KW)[:, t_len - wb:], st(OFF_VW)[:, t_len - wb:],
              st(OFF_KD), st(OFF_VD), mk3.reshape(bsz, mlen, MX_H, HD), mv3.reshape(bsz, mlen, MX_H, HD),
              s_fin, shift, ulast[:, 8 - (CONV_K - 1):])
    return x_out, states


def _sample_layer(x2d, c, page_table, lw, pl_, bsz, t_len, past_len):
    z, _ = _proj(x2d, pl_['g_mix'], pl_['w_pad'], pl_['gains'], pl_['m64'], pl_['m32'])
    y_nsa, y_df = _sample_attn(z, c, page_table, lw, pl_['lam'][0], pl_['lam_init'], bsz, t_len, past_len)
    shift0 = jnp.zeros((bsz, 8, RW_P), F32).at[:, 7].set(c['rwkv_shift'])
    conv0 = jnp.zeros((bsz, 8, 256), F32).at[:, 8 - (CONV_K - 1):].set(c['conv'])
    seq = _seq(z, shift0, conv0, pl_['seqw'], bsz, t_len)
    r, w, k, v, am, bb, g, y_cv, ulast = seq
    y_scan, s_fin = _rwkv_scan((r, w, k, v, am, bb), c['rwkv'], bsz, t_len)
    mk = c['mem_k'].reshape(bsz, -1, 256).astype(BF16)
    mv = c['mem_v'].reshape(bsz, -1, 256).astype(BF16)
    x_out = _tail(x2d, z, (y_nsa, y_scan, y_cv, y_df), (r, k, v, g), mk, mv, pl_, bsz, t_len)
    st = lambda off: z[:, off:off + 256].reshape(bsz, t_len, NSA_H, HD)
    shift = z[:, OFF_RW:OFF_RW + RW_P].reshape(bsz, t_len, RW_P)[:, -1]
    states = (st(OFF_KC), st(OFF_VC), st(OFF_KS), st(OFF_VS), st(OFF_KW), st(OFF_VW), st(OFF_KD), st(OFF_VD),
              s_fin, shift, ulast[:, 8 - (CONV_K - 1):])
    return x_out, states


def kernel(x_prompt, x_sample, mem_prompt, cache_cmp_k, cache_cmp_v, cache_sel_k, cache_sel_v, cache_win_k, cache_win_v, cache_diff_k, cache_diff_v, cache_mem_k, cache_mem_v, state_rwkv, state_rwkv_shift, state_conv, page_table, norm_mix, w_in, nsa_qn, nsa_kn, nsa_ck_w, nsa_ck_pe, nsa_cv_w, nsa_cv_pe, rw_mu, rw_w0, rw_w2, rw_a0, rw_a2, rw_g2, rw_kk, rw_ka, rw_rk, rw_lnw, rw_lnb, conv_w, conv_b, df_qn, df_kn, df_lq1, df_lk1, df_lq2, df_lk2, df_subln, w_br, w_o, norm_x, norm_mem, w_xq, w_xk, w_xv, x_qn, x_kn, w_xo, norm_ffn, pk_wq, pk_keys, pk_u, pk_v):
    bp, tp, _ = x_prompt.shape
    bs, ts, _ = x_sample.shape
    depth = w_in.shape[0]
    past_len = page_table.shape[1] * PAGE_SIZE
    weights = dict(norm_mix=norm_mix, w_in=w_in, nsa_qn=nsa_qn, nsa_kn=nsa_kn, nsa_ck_w=nsa_ck_w,
                   nsa_ck_pe=nsa_ck_pe, nsa_cv_w=nsa_cv_w, nsa_cv_pe=nsa_cv_pe, rw_mu=rw_mu, rw_w0=rw_w0,
                   rw_w2=rw_w2, rw_a0=rw_a0, rw_a2=rw_a2, rw_g2=rw_g2, rw_kk=rw_kk, rw_ka=rw_ka, rw_rk=rw_rk,
                   rw_lnw=rw_lnw, rw_lnb=rw_lnb, conv_w=conv_w, conv_b=conv_b, df_qn=df_qn, df_kn=df_kn,
                   df_lq1=df_lq1, df_lk1=df_lk1, df_lq2=df_lq2, df_lk2=df_lk2, df_subln=df_subln, w_br=w_br,
                   w_o=w_o, norm_x=norm_x, norm_mem=norm_mem, w_xq=w_xq, w_xk=w_xk, w_xv=w_xv, x_qn=x_qn,
                   x_kn=x_kn, w_xo=w_xo, norm_ffn=norm_ffn, pk_wq=pk_wq, pk_keys=pk_keys, pk_u=pk_u, pk_v=pk_v)
    xp = x_prompt.reshape(bp * tp, D_MODEL)
    xs = x_sample.reshape(bs * ts, D_MODEL)
    mem2d = mem_prompt.reshape(-1, D_MODEL)
    p_acc, s_acc = [], []
    for l in range(depth):
        lw = {k_: v_[l] for k_, v_ in weights.items()}
        pl_ = _prep_layer(lw, l)
        c = dict(cmp_k=cache_cmp_k[l], cmp_v=cache_cmp_v[l], sel_k=cache_sel_k[l], sel_v=cache_sel_v[l],
                 win_k=cache_win_k[l], win_v=cache_win_v[l], diff_k=cache_diff_k[l], diff_v=cache_diff_v[l],
                 mem_k=cache_mem_k[l], mem_v=cache_mem_v[l], rwkv=state_rwkv[l],
                 rwkv_shift=state_rwkv_shift[l], conv=state_conv[l])
        xp, ps = _prompt_layer(xp, mem2d, pl_, bp, tp)
        xs, ss = _sample_layer(xs, c, page_table, lw, pl_, bs, ts, past_len)
        p_acc.append(ps)
        s_acc.append(ss)
    p_out = [jnp.stack(a) for a in zip(*p_acc)]
    s_out = [jnp.stack(a) for a in zip(*s_acc)]
    return (xp.reshape(bp, tp, D_MODEL), xs.reshape(bs, ts, D_MODEL), *p_out, *s_out)
```

```python
import functools
import math

import jax
import jax.numpy as jnp
import numpy as np
from jax import lax
from jax.experimental import pallas as pl
from jax.experimental.pallas import tpu as pltpu

F32 = jnp.float32
BF16 = jnp.bfloat16
I32 = jnp.int32

D_MODEL = 1024
PAGE_SIZE = 128
N_BRANCH = 4
MIX_W = 256
NSA_H = 4
HD = 64
CMP_BLOCK = 32
SEL_BLOCK = 64
SEL_TOPK = 16
WINDOW = 512
RW_H = 4
RW_N = 64
DECAY_LORA = 32
AAA_LORA = 32
GATE_LORA = 64
RW_P = 896
CONV_K = 3
DF_H = 4
DF_DQK = 32
DF_DV = 64
MX_H = 4
PK_H = 8
N_KEYS = 128
PK_DK = 128
PK_TOPK = 16
EPS = 1e-6
GN_EPS = 64e-5
N_ALIBI = NSA_H + DF_H
ALIBI_SLOPES = tuple(2.0 ** (-8.0 * (i + 1) / N_ALIBI) for i in range(N_ALIBI))
NSA_SLOPES = ALIBI_SLOPES[0::2]
DF_SLOPES = ALIBI_SLOPES[1::2]

LANES = 128
VMEM_LIMIT = 56 * 1024 * 1024

OFF_GATE = 0
OFF_VD = 4096
OFF_NG = 4352
OFF_RW = 4480
OFF_Q = 5376
OFF_KS = 5632
OFF_KW = 5888
OFF_QD = 6144
OFF_KD = 6400
OFF_KC = 6656
OFF_VC = 6912
OFF_VS = 7168
OFF_VW = 7424
OFF_BG = 7680
OFF_CG = 7936
OFF_XC = 8192
P_PAD = 8448
ZB_KS, ZB_KW, ZB_KD, ZB_VS, ZB_VW, ZB_VD = range(6)


def _dot(a, b):
    return jnp.dot(a, b, preferred_element_type=F32)


def _dot_nt(a, b):
    return lax.dot_general(a, b, (((1,), (1,)), ((), ())), preferred_element_type=F32)


def _seg_sum(x, m):
    hi = x.astype(BF16)
    lo = (x - hi.astype(F32)).astype(BF16)
    return _dot(hi, m) + _dot(lo, m)


def _iota(shape, dim):
    return lax.broadcasted_iota(I32, shape, dim)


def _row_tile(n, pref):
    t = pref
    while t > 8 and n % t:
        t //= 2
    assert n % t == 0, (n, pref)
    return t


def _cparams(sem):
    return pltpu.CompilerParams(dimension_semantics=sem, vmem_limit_bytes=VMEM_LIMIT)


def _vmem_full():
    return pl.BlockSpec(memory_space=pltpu.VMEM)


_NORMED = ((OFF_Q, 64), (OFF_KS, 64), (OFF_KW, 64), (OFF_QD, 32), (OFF_KD, 32))
_ZB_SRC = ((OFF_KS, ZB_KS), (OFF_KW, ZB_KW), (OFF_KD, ZB_KD), (OFF_VS, ZB_VS), (OFF_VW, ZB_VW), (OFF_VD, ZB_VD))


def _proj_plan():
    zb_of = dict(_ZB_SRC)
    normed = dict(_NORMED)
    special = sorted(set(zb_of) | set(normed))
    plan, c0 = [], 0
    for s in special + [P_PAD]:
        while c0 < s:
            w = min(512, s - c0)
            plan.append((c0, w, 0, -1))
            c0 += w
        if s < P_PAD:
            plan.append((s, 256, normed.get(s, 0), zb_of.get(s, -1)))
            c0 = s + 256
    return plan


def _proj_kernel(x_ref, g_ref, w_ref, gains_ref, m64_ref, m32_ref, z_ref, zb_ref):
    x = x_ref[...]
    h = x * lax.rsqrt(jnp.mean(x * x, axis=-1, keepdims=True) + EPS) * g_ref[...]
    hb = h.astype(BF16)
    norm_row = {o: r for r, (o, _) in enumerate(_NORMED)}
    for c0, w, gs, slot in _proj_plan():
        z = _dot(hb, w_ref[:, c0:c0 + w])
        if gs:
            m = m64_ref[...] if gs == 64 else m32_ref[...]
            ss = _seg_sum(z * z, m)
            r = norm_row[c0]
            z = z * lax.rsqrt(ss * (1.0 / gs) + EPS) * gains_ref[r:r + 1, :]
        z_ref[:, c0:c0 + w] = z
        if slot >= 0:
            zb_ref[:, slot * 256:(slot + 1) * 256] = z.astype(BF16)


def _proj(x2d, g, w_pad, gains, m64, m32):
    n = x2d.shape[0]
    tm = _row_tile(n, 256)
    return pl.pallas_call(
        _proj_kernel,
        grid=(n // tm,),
        in_specs=[pl.BlockSpec((tm, D_MODEL), lambda i: (i, 0)),
                  pl.BlockSpec((1, D_MODEL), lambda i: (0, 0)),
                  _vmem_full(),
                  pl.BlockSpec((8, 256), lambda i: (0, 0)),
                  pl.BlockSpec((256, 256), lambda i: (0, 0)),
                  pl.BlockSpec((256, 256), lambda i: (0, 0))],
        out_specs=[pl.BlockSpec((tm, P_PAD), lambda i: (i, 0)),
                   pl.BlockSpec((tm, 1536), lambda i: (i, 0))],
        out_shape=[jax.ShapeDtypeStruct((n, P_PAD), F32), jax.ShapeDtypeStruct((n, 1536), BF16)],
        compiler_params=_cparams(("parallel",)),
    )(x2d, g, w_pad, gains, m64, m32)


CMP_K = CMP_BLOCK * 256


def _cmp_kernel(kc_ref, vc_ref, wk_ref, wv_ref, pek_ref, pev_ref, m64_ref, gk_ref,
                kce_ref, kco_ref, vce_ref, vco_ref):
    for src, w_ref, pe_ref, outs, is_k in ((kc_ref, wk_ref, pek_ref, (kce_ref, kco_ref), True),
                                           (vc_ref, wv_ref, pev_ref, (vce_ref, vco_ref), False)):
        for par in range(2):
            rows = src[:, par * CMP_K:(par + 1) * CMP_K] + pe_ref[...]
            acc = _dot(rows.astype(BF16), w_ref[...])
            if is_k:
                ss = _seg_sum(acc * acc, m64_ref[...])
                acc = acc * lax.rsqrt(ss * (1.0 / HD) + EPS) * gk_ref[...]
            outs[par][...] = acc.astype(BF16)


def _cmp(kc2, vc2, wk_bd, wv_bd, pek, pev, m64, gk):
    nr = kc2.shape[0]
    rb = _row_tile(nr, 64)
    out = jax.ShapeDtypeStruct((nr, 256), BF16)
    ospec = pl.BlockSpec((rb, 256), lambda i: (i, 0))
    ispec = pl.BlockSpec((rb, 2 * CMP_K), lambda i: (i, 0))
    par = lambda s: pl.BlockSpec(s, lambda i: (0,) * len(s))
    return pl.pallas_call(
        _cmp_kernel,
        grid=(nr // rb,),
        in_specs=[ispec, ispec, par((CMP_K, 256)), par((CMP_K, 256)), par((1, CMP_K)), par((1, CMP_K)),
                  par((256, 256)), par((1, 256))],
        out_specs=[ospec, ospec, ospec, ospec],
        out_shape=[out, out, out, out],
        compiler_params=_cparams(("parallel",)),
    )(kc2, vc2, wk_bd, wv_bd, pek, pev, m64, gk)


def _softmax_rows(s):
    m = jnp.max(s, axis=-1, keepdims=True)
    e = jnp.exp(s - m)
    return e / jnp.sum(e, axis=-1, keepdims=True)


def _nsa_kernel(q_ref, ng_ref, kce_ref, kco_ref, vce_ref, vco_ref, ks_ref, vs_ref, kw_ref, vw_ref, ex_ref,
                o_ref, *, t_len, tq, kp, band):
    i = pl.program_id(1)
    nsb = t_len // SEL_BLOCK
    q = q_ref[...]
    sig = jax.nn.sigmoid(ng_ref[...])
    lane = _iota((1, 256), 1)
    qpos = i * tq + _iota((tq, 1), 0)
    cur = jnp.right_shift(qpos, 6)
    jb = _iota((1, nsb), 1)
    kpos = _iota((1, t_len), 1)
    dist = (qpos - kpos).astype(F32)
    curf = jnp.where(jnp.right_shift(kpos, 6) == cur, jnp.where(kpos <= qpos, 1.0, 0.0), 0.0)
    de = (qpos - (jb * SEL_BLOCK + (CMP_BLOCK - 1))).astype(F32)
    do = (qpos - (jb * SEL_BLOCK + (SEL_BLOCK - 1))).astype(F32)
    past = jb < cur
    start = jnp.clip(i * tq - WINDOW, 0, t_len - band)
    start = pl.multiple_of(start, tq)
    kposw = start + _iota((1, band), 1)
    dw = qpos - kposw
    okw = jnp.where(dw >= 0, jnp.where(dw < WINDOW, 1.0, 0.0), 0.0) > 0.5
    dwf = dw.astype(F32)
    kwb = kw_ref[pl.ds(start, band), :]
    vwb = vw_ref[pl.ds(start, band), :]
    acc = jnp.zeros((tq, 256), F32)
    for h in range(NSA_H):
        hm = jnp.right_shift(lane, 6) == h
        qh = jnp.where(hm, q, 0.0).astype(BF16)
        sl = NSA_SLOPES[h]
        se = jnp.where(de >= 0, _dot_nt(qh, kce_ref[...]) - sl * de, -jnp.inf)
        so = jnp.where(do >= 0, _dot_nt(qh, kco_ref[...]) - sl * do, -jnp.inf)
        m = jnp.maximum(jnp.max(se, axis=-1, keepdims=True), jnp.max(so, axis=-1, keepdims=True))
        m = jnp.where(m > -jnp.inf, m, 0.0)
        pe = jnp.where(de >= 0, jnp.exp(se - m), 0.0)
        po = jnp.where(do >= 0, jnp.exp(so - m), 0.0)
        den = jnp.sum(pe, axis=-1, keepdims=True) + jnp.sum(po, axis=-1, keepdims=True)
        den = jnp.where(den > 0, den, 1.0)
        pe = pe / den
        po = po / den
        o_c = _dot(pe.astype(BF16), vce_ref[...]) + _dot(po.astype(BF16), vco_ref[...])
        imp = jnp.where(past, pe + po, -1.0)
        rank = jnp.zeros((tq, nsb), F32)
        for j2 in range(nsb):
            col = imp[:, j2:j2 + 1]
            before = jnp.where(jb > j2, 1.0, 0.0)
            rank = rank + jnp.where(col > imp, 1.0, jnp.where(col == imp, before, 0.0))
        member = jnp.where(past, jnp.where(rank < kp, 1.0, 0.0), 0.0)
        allowed = (_dot(member.astype(BF16), ex_ref[...]) + curf) > 0.5
        s = jnp.where(allowed, _dot_nt(qh, ks_ref[...]) - sl * dist, -jnp.inf)
        o_s = _dot(_softmax_rows(s).astype(BF16), vs_ref[...])
        s = jnp.where(okw, _dot_nt(qh, kwb) - sl * dwf, -jnp.inf)
        o_w = _dot(_softmax_rows(s).astype(BF16), vwb)
        y = (sig[:, h:h + 1] * o_c + sig[:, NSA_H + h:NSA_H + h + 1] * o_s
             + sig[:, 2 * NSA_H + h:2 * NSA_H + h + 1] * o_w)
        acc = acc + jnp.where(hm, y, 0.0)
    o_ref[...] = acc


def _nsa_prompt(z, zb, cmp4, expand, bsz, t_len):
    tq = 128
    nt = t_len // tq
    nsb = t_len // SEL_BLOCK
    kp = min(SEL_TOPK - 1, nsb)
    band = min(WINDOW + tq, t_len)
    kv = lambda c: pl.BlockSpec((t_len, 256), lambda b, i: (b, c))
    cs = pl.BlockSpec((nsb, 256), lambda b, i: (b, 0))
    return pl.pallas_call(
        functools.partial(_nsa_kernel, t_len=t_len, tq=tq, kp=kp, band=band),
        grid=(bsz, nt),
        in_specs=[pl.BlockSpec((tq, 256), lambda b, i: (b * nt + i, OFF_Q // 256)),
                  pl.BlockSpec((tq, 128), lambda b, i: (b * nt + i, OFF_NG // 128)),
                  cs, cs, cs, cs, kv(ZB_KS), kv(ZB_VS), kv(ZB_KW), kv(ZB_VW),
                  pl.BlockSpec((nsb, t_len), lambda b, i: (0, 0))],
        out_specs=pl.BlockSpec((tq, 256), lambda b, i: (b * nt + i, 0)),
        out_shape=jax.ShapeDtypeStruct((bsz * t_len, 256), F32),
        compiler_params=_cparams(("parallel", "parallel")),
    )(z, z, *cmp4, zb, zb, zb, zb, expand)


def _diff_kernel(lam_ref, q_ref, k_ref, v_ref, m64_ref, gain_ref, o_ref, *, t_len, tq):
    i = pl.program_id(1)
    lam = lam_ref[0]
    q = q_ref[...]
    lane = _iota((1, 256), 1)
    qpos = i * tq + _iota((tq, 1), 0)
    kpos = _iota((1, t_len), 1)
    causal = kpos <= qpos
    dist = (qpos - kpos).astype(F32)
    acc = jnp.zeros((tq, 256), F32)
    for h in range(DF_H):
        ps = []
        for c in range(2):
            lm = jnp.right_shift(lane, 5) == (2 * h + c)
            qh = jnp.where(lm, q, 0.0).astype(BF16)
            s = jnp.where(causal, _dot_nt(qh, k_ref[...]) - DF_SLOPES[h] * dist, -jnp.inf)
            ps.append(_softmax_rows(s))
        pd = (ps[0] - lam * ps[1]).astype(BF16)
        acc = acc + jnp.where(jnp.right_shift(lane, 6) == h, _dot(pd, v_ref[...]), 0.0)
    ss = _seg_sum(acc * acc, m64_ref[...])
    o_ref[...] = acc * lax.rsqrt(ss * (1.0 / DF_DV) + EPS) * gain_ref[...]


def _diff_prompt(lam, z, zb, m64, gain, bsz, t_len):
    tq = 128
    nt = t_len // tq
    kv = lambda c: pl.BlockSpec((t_len, 256), lambda b, i: (b, c))
    return pl.pallas_call(
        functools.partial(_diff_kernel, t_len=t_len, tq=tq),
        grid=(bsz, nt),
        in_specs=[pl.BlockSpec(memory_space=pltpu.SMEM),
                  pl.BlockSpec((tq, 256), lambda b, i: (b * nt + i, OFF_QD // 256)),
                  kv(ZB_KD), kv(ZB_VD),
                  pl.BlockSpec((256, 256), lambda b, i: (0, 0)),
                  pl.BlockSpec((1, 256), lambda b, i: (0, 0))],
        out_specs=pl.BlockSpec((tq, 256), lambda b, i: (b * nt + i, 0)),
        out_shape=jax.ShapeDtypeStruct((bsz * t_len, 256), F32),
        compiler_params=_cparams(("parallel", "parallel")),
    )(lam, z, zb, zb, m64, gain)


def _shift_rows(x, first_rows, k):
    n = x.shape[0]
    rolled = pltpu.roll(x, k, 0)
    row = _iota((n, 1), 0)
    out = rolled
    for r in range(k):
        out = jnp.where(row == r, first_rows[8 - k + r:8 - k + r + 1, :], out)
    return out


def _seq_kernel(rw_ref, rwp_ref, bg_ref, cg_ref, xc_ref, cgp_ref, xcp_ref, sh0_ref, cv0_ref,
                mu_ref, w0_ref, a0_ref, lw_ref, la_ref, lg_ref, kk_ref, ka_ref, cw_ref, cb_ref, m64_ref,
                r_ref, w_ref, k_ref, v_ref, am_ref, bb_ref, g_ref, ycv_ref, ulast_ref, *, ts):
    i = pl.program_id(1)
    x = rw_ref[...]
    first = jnp.where(i == 0, sh0_ref[0], rwp_ref[...])
    prev = _shift_rows(x, first, 1)
    xm = x + (prev - x) * mu_ref[...]
    r = xm[:, 0:256]
    k = xm[:, 256:512]
    v = xm[:, 512:768]
    lo = xm[:, 768:896]
    wlin = w0_ref[...] + _dot(jnp.tanh(lo).astype(BF16), lw_ref[...])
    w = -jax.nn.softplus(-wlin) - 0.5
    decay = jnp.exp(-jnp.exp(w))
    a = jax.nn.sigmoid(a0_ref[...] + _dot(lo.astype(BF16), la_ref[...]))
    g = _dot(jax.nn.sigmoid(lo).astype(BF16), lg_ref[...])
    kk = k * kk_ref[...]
    nrm = jnp.sqrt(_seg_sum(kk * kk, m64_ref[...]))
    kk = kk / jnp.maximum(nrm, 1e-12)
    r_ref[...] = r
    w_ref[...] = decay
    k_ref[...] = k * (1.0 + (a - 1.0) * ka_ref[...])
    v_ref[...] = v
    am_ref[...] = -kk
    bb_ref[...] = kk * a
    g_ref[...] = g
    u = cg_ref[...] * xc_ref[...]
    ufirst = jnp.where(i == 0, cv0_ref[0], cgp_ref[...] * xcp_ref[...])
    u1 = _shift_rows(u, ufirst, 1)
    u2 = _shift_rows(u, ufirst, 2)
    y = cb_ref[...] + u2 * cw_ref[0:1, :] + u1 * cw_ref[1:2, :] + u * cw_ref[2:3, :]
    ycv_ref[...] = bg_ref[...] * y
    ulast_ref[0] = u[ts - 8:ts, :]


def _seq(z, shift0, conv0, lwp, bsz, t_len):
    ts = _row_tile(t_len, 512)
    nt = t_len // ts
    n = bsz * t_len
    row = lambda w, c: pl.BlockSpec((ts, w), lambda b, i: (b * nt + i, c))
    prv = lambda w, c: pl.BlockSpec((8, w), lambda b, i: (jnp.maximum((b * nt + i) * (ts // 8) - 1, 0), c))
    par = lambda s: pl.BlockSpec(s, lambda b, i: (0,) * len(s))
    o256 = pl.BlockSpec((ts, 256), lambda b, i: (b * nt + i, 0))
    sd = jax.ShapeDtypeStruct((n, 256), F32)
    return pl.pallas_call(
        functools.partial(_seq_kernel, ts=ts),
        grid=(bsz, nt),
        in_specs=[row(RW_P, OFF_RW // RW_P), prv(RW_P, OFF_RW // RW_P),
                  row(256, OFF_BG // 256), row(256, OFF_CG // 256), row(256, OFF_XC // 256),
                  prv(256, OFF_CG // 256), prv(256, OFF_XC // 256),
                  pl.BlockSpec((1, 8, RW_P), lambda b, i: (b, 0, 0)),
                  pl.BlockSpec((1, 8, 256), lambda b, i: (b, 0, 0)),
                  par((1, RW_P)), par((1, 256)), par((1, 256)), par((128, 256)), par((128, 256)), par((128, 256)),
                  par((1, 256)), par((1, 256)), par((8, 256)), par((1, 256)), par((256, 256))],
        out_specs=[o256] * 8 + [pl.BlockSpec((1, 8, 256), lambda b, i: (b, 0, 0))],
        out_shape=[sd] * 8 + [jax.ShapeDtypeStruct((bsz, 8, 256), F32)],
        compiler_params=_cparams(("parallel", "arbitrary")),
    )(z, z, z, z, z, z, z, shift0, conv0, lwp['mu'], lwp['w0'], lwp['a0'], lwp['lw'], lwp['la'], lwp['lg'],
      lwp['kk'], lwp['ka'], lwp['cw'], lwp['cb'], lwp['m64'])


def _scan_kernel(s0_ref, am_ref, w_ref, bb_ref, k_ref, r_ref, v_ref, y_ref, s_ref, *, tc, ng, ih):
    @pl.when(pl.program_id(0) == 0)
    def _():
        s_ref[...] = s0_ref[...]

    def step(t, carry):
        def group(g, c2):
            h = g // ih
            hs = pl.multiple_of(h * RW_N, RW_N)
            st = s_ref[g]
            a_c = am_ref[t, pl.ds(hs, RW_N), :]
            sa = jnp.sum(st * a_c, axis=0, keepdims=True)
            vrow = v_ref[t, pl.ds(g, 1), :]
            st = (st * w_ref[t, pl.ds(hs, RW_N), :] + sa * bb_ref[t, pl.ds(hs, RW_N), :]
                  + vrow * k_ref[t, pl.ds(hs, RW_N), :])
            s_ref[g] = st
            y_ref[t, pl.ds(g, 1), :] = jnp.sum(st * r_ref[t, pl.ds(hs, RW_N), :], axis=0, keepdims=True)
            return c2

        return lax.fori_loop(0, ng, group, carry, unroll=4)

    lax.fori_loop(0, tc, step, 0)


def _scan(s0, am, w, bb, k, r, v, t_len, ih):
    ng = RW_H * ih
    tc = _row_tile(t_len, 16)
    col = pl.BlockSpec((tc, 256, LANES), lambda i: (i, 0, 0))
    rowv = pl.BlockSpec((tc, ng, LANES), lambda i: (i, 0, 0))
    st = pl.BlockSpec((ng, RW_N, LANES), lambda i: (0, 0, 0))
    return pl.pallas_call(
        functools.partial(_scan_kernel, tc=tc, ng=ng, ih=ih),
        grid=(t_len // tc,),
        in_specs=[st, col, col, col, col, col, rowv],
        out_specs=[rowv, st],
        out_shape=[jax.ShapeDtypeStruct((t_len, ng, LANES), F32), jax.ShapeDtypeStruct((ng, RW_N, LANES), F32)],
        compiler_params=_cparams(("arbitrary",)),
    )(s0, am, w, bb, k, r, v)


def _to_cols(x, bsz, t_len, il):
    x = x.reshape(bsz, t_len, 256).transpose(1, 2, 0)
    return jnp.tile(x, (1, 1, il))


def _to_rows(x, bsz, t_len, il, ih):
    x = x.reshape(bsz, t_len, RW_H, ih, il).transpose(1, 2, 3, 4, 0)
    return x.reshape(t_len, RW_H * ih, il * bsz)


def _from_rows(y, bsz, t_len, il, ih):
    y = y.reshape(t_len, RW_H, ih, il, bsz).transpose(4, 0, 1, 2, 3)
    return y.reshape(bsz * t_len, 256)


def _state_in(s, bsz, il, ih):
    s = s.reshape(bsz, RW_H, ih, il, RW_N).transpose(1, 2, 4, 3, 0)
    return s.reshape(RW_H * ih, RW_N, il * bsz)


def _state_out(s, bsz, il, ih):
    s = s.reshape(RW_H, ih, RW_N, il, bsz).transpose(4, 0, 1, 3, 2)
    return s.reshape(bsz, RW_H, RW_N, RW_N)


def _rwkv_scan(seq_out, s0, bsz, t_len):
    r, w, k, v, am, bb = seq_out
    il = LANES // bsz
    ih = RW_N // il
    cols = [_to_cols(a, bsz, t_len, il) for a in (am, w, bb, k, r)]
    y, s_fin = _scan(_state_in(s0, bsz, il, ih), *cols, _to_rows(v, bsz, t_len, il, ih), t_len, ih)
    return _from_rows(y, bsz, t_len, il, ih), _state_out(s_fin, bsz, il, ih)


def _merge_kernel(x_ref, gate_ref, ynsa_ref, yrw_ref, r_ref, k_ref, v_ref, g_ref, ycv_ref, ydf_ref,
                  wbr_ref, wo_ref, lnw_ref, lnb_ref, rk_ref, m64_ref, nx_ref, wxq_ref, qg_ref,
                  x1_ref, qx_ref):
    m64 = m64_ref[...]
    y = yrw_ref[...]
    mu = _seg_sum(y, m64) * (1.0 / RW_N)
    d = y - mu
    var = _seg_sum(d * d, m64) * (1.0 / RW_N)
    yn = d * lax.rsqrt(var + GN_EPS) * lnw_ref[...] + lnb_ref[...]
    v = v_ref[...]
    bonus = _seg_sum(r_ref[...] * k_ref[...] * rk_ref[...], m64) * v
    y_rw = (yn + bonus) * g_ref[...]
    ys = (ynsa_ref[...], y_rw, ycv_ref[...], ydf_ref[...])
    zsum = None
    for b in range(N_BRANCH):
        t = jax.nn.sigmoid(gate_ref[:, b * D_MODEL:(b + 1) * D_MODEL]) * _dot(ys[b].astype(BF16), wbr_ref[b])
        zsum = t if zsum is None else zsum + t
    x1 = x_ref[...] + _dot(zsum.astype(BF16), wo_ref[...])
    x1_ref[...] = x1
    hx = x1 * lax.rsqrt(jnp.mean(x1 * x1, axis=-1, keepdims=True) + EPS) * nx_ref[...]
    qx = _dot(hx.astype(BF16), wxq_ref[...])
    ss = _seg_sum(qx * qx, m64)
    qx_ref[...] = qx * lax.rsqrt(ss * (1.0 / HD) + EPS) * qg_ref[...]


def _merge(x2d, z, ynsa, yrw, r, k, v, g, ycv, ydf, mw):
    n = x2d.shape[0]
    tm = _row_tile(n, 256)
    row = lambda w, c=0: pl.BlockSpec((tm, w), lambda i: (i, c))
    par = lambda s: pl.BlockSpec(s, lambda i: (0,) * len(s))
    return pl.pallas_call(
        _merge_kernel,
        grid=(n // tm,),
        in_specs=[row(D_MODEL), row(4 * D_MODEL, OFF_GATE)] + [row(256)] * 8
                 + [par((4, 256, D_MODEL)), par((D_MODEL, D_MODEL)), par((1, 256)), par((1, 256)), par((1, 256)),
                    par((256, 256)), par((1, D_MODEL)), par((D_MODEL, 256)), par((1, 256))],
        out_specs=[row(D_MODEL), row(256)],
        out_shape=[jax.ShapeDtypeStruct((n, D_MODEL), F32), jax.ShapeDtypeStruct((n, 256), F32)],
        compiler_params=_cparams(("parallel",)),
    )(x2d, z, ynsa, yrw, r, k, v, g, ycv, ydf, mw['wbr'], mw['wo'], mw['lnw'], mw['lnb'], mw['rk'], mw['m64'],
      mw['nx'], mw['wxq'], mw['qg'])


def _memkv_kernel(x_ref, g_ref, w_ref, m64_ref, kg_ref, k_ref, v_ref):
    x = x_ref[...]
    h = (x * lax.rsqrt(jnp.mean(x * x, axis=-1, keepdims=True) + EPS) * g_ref[...]).astype(BF16)
    k = _dot(h, w_ref[:, 0:256])
    ss = _seg_sum(k * k, m64_ref[...])
    k_ref[...] = k * lax.rsqrt(ss * (1.0 / HD) + EPS) * kg_ref[...]
    v_ref[...] = _dot(h, w_ref[:, 256:512])


def _memkv(mem2d, g, wkv, m64, kg):
    n = mem2d.shape[0]
    tm = _row_tile(n, 256)
    par = lambda s: pl.BlockSpec(s, lambda i: (0,) * len(s))
    row = lambda w: pl.BlockSpec((tm, w), lambda i: (i, 0))
    sd = jax.ShapeDtypeStruct((n, 256), F32)
    return pl.pallas_call(
        _memkv_kernel,
        grid=(n // tm,),
        in_specs=[row(D_MODEL), par((1, D_MODEL)), par((D_MODEL, 512)), par((256, 256)), par((1, 256))],
        out_specs=[row(256), row(256)],
        out_shape=[sd, sd],
        compiler_params=_cparams(("parallel",)),
    )(mem2d, g, wkv, m64, kg)


def _xattn_kernel(qx_ref, mk_ref, mv_ref, x1_ref, wxo_ref, nf_ref, x2_ref, hf_ref):
    q = qx_ref[...]
    lane = _iota((1, 256), 1)
    mk = mk_ref[0]
    mv = mv_ref[0]
    acc = jnp.zeros(q.shape, F32)
    for h in range(MX_H):
        hm = jnp.right_shift(lane, 6) == h
        s = _dot_nt(jnp.where(hm, q, 0.0).astype(BF16), mk)
        acc = acc + jnp.where(hm, _dot(_softmax_rows(s).astype(BF16), mv), 0.0)
    x2 = x1_ref[...] + _dot(acc.astype(BF16), wxo_ref[...])
    x2_ref[...] = x2
    hf_ref[...] = x2 * lax.rsqrt(jnp.mean(x2 * x2, axis=-1, keepdims=True) + EPS) * nf_ref[...]


def _xattn(qx, mk, mv, x1, wxo, nf, bsz, t_len):
    tq = _row_tile(t_len, 256)
    nt = t_len // tq
    mlen = mk.shape[1]
    row = lambda w: pl.BlockSpec((tq, w), lambda b, i: (b * nt + i, 0))
    mem = pl.BlockSpec((1, mlen, 256), lambda b, i: (b, 0, 0))
    sd = jax.ShapeDtypeStruct((bsz * t_len, D_MODEL), F32)
    return pl.pallas_call(
        _xattn_kernel,
        grid=(bsz, nt),
        in_specs=[row(256), mem, mem, row(D_MODEL),
                  pl.BlockSpec((256, D_MODEL), lambda b, i: (0, 0)),
                  pl.BlockSpec((1, D_MODEL), lambda b, i: (0, 0))],
        out_specs=[row(D_MODEL), row(D_MODEL)],
        out_shape=[sd, sd],
        compiler_params=_cparams(("parallel", "parallel")),
    )(qx, mk, mv, x1, wxo, nf)


PEER_TILE = 128


def _topk_cols(s, ids, n_take, payload=None):
    big = PK_TOPK * PK_TOPK + N_KEYS
    vals, picks = [], []
    for _ in range(n_take):
        m = jnp.max(s, axis=0, keepdims=True)
        ix = jnp.min(jnp.where(s == m, ids, big), axis=0, keepdims=True)
        hit = ids == ix
        vals.append(m)
        picks.append(ix if payload is None else jnp.max(jnp.where(hit, payload, -1), axis=0, keepdims=True))
        s = jnp.where(hit, -jnp.inf, s)
    return jnp.concatenate(vals, axis=0), jnp.concatenate(picks, axis=0)


def _peer_score_kernel(hf_ref, wq_ref, kbd_ref, e_ref, g_ref):
    q = _dot(hf_ref[...].astype(BF16), wq_ref[...])
    st = _dot_nt(kbd_ref[...], q.astype(BF16))
    kid = _iota((N_KEYS, PEER_TILE), 0)
    half = PK_TOPK // 2
    cids, cmask = [], []
    for a in range(half):
        nb = PK_TOPK // (a + 1)
        rows = -(-nb // 8) * 8
        b_io = _iota((rows, PEER_TILE), 0)
        cids.append(a * PK_TOPK + b_io)
        cmask.append(b_io < nb)
    cids.append((half + _iota((half, PEER_TILE), 0)) * PK_TOPK)
    cid = jnp.concatenate(cids, axis=0)
    for h in range(PK_H):
        sv0, si0 = _topk_cols(st[(2 * h) * N_KEYS:(2 * h + 1) * N_KEYS, :], kid, PK_TOPK)
        sv1, si1 = _topk_cols(st[(2 * h + 1) * N_KEYS:(2 * h + 2) * N_KEYS, :], kid, PK_TOPK)
        cparts, eparts = [], []
        for a in range(half):
            rows = cids[a].shape[0]
            cparts.append(jnp.where(cmask[a], sv0[a:a + 1, :] + sv1[0:rows, :], -jnp.inf))
            eparts.append(si0[a:a + 1, :] * N_KEYS + si1[0:rows, :])
        cparts.append(sv0[half:, :] + sv1[0:1, :])
        eparts.append(si0[half:, :] * N_KEYS + si1[0:1, :])
        cand = jnp.concatenate(cparts, axis=0)
        ecand = jnp.concatenate(eparts, axis=0)
        cv, ev = _topk_cols(cand, cid, PK_TOPK, payload=ecand)
        ex = jnp.exp(cv - cv[0:1, :])
        g_ref[h * PK_TOPK:(h + 1) * PK_TOPK, :] = ex / jnp.sum(ex, axis=0, keepdims=True)
        e_ref[h * PK_TOPK:(h + 1) * PK_TOPK, :] = ev


def _peer_score(hf, wq, kbd):
    n = hf.shape[0]
    nt = n // PEER_TILE
    nsel = PK_H * PK_TOPK
    out = pl.BlockSpec((nsel, PEER_TILE), lambda i: (i, 0))
    return pl.pallas_call(
        _peer_score_kernel,
        grid=(nt,),
        in_specs=[pl.BlockSpec((PEER_TILE, D_MODEL), lambda i: (i, 0)),
                  pl.BlockSpec((D_MODEL, PK_H * PK_DK), lambda i: (0, 0)),
                  pl.BlockSpec((PK_H * 2 * N_KEYS, PK_H * PK_DK), lambda i: (0, 0))],
        out_specs=[out, out],
        out_shape=[jax.ShapeDtypeStruct((nt * nsel, PEER_TILE), I32),
                   jax.ShapeDtypeStruct((nt * nsel, PEER_TILE), F32)],
        compiler_params=_cparams(("parallel",)),
    )(hf, wq, kbd)


def _unpack_pair(wd):
    lo = pltpu.bitcast(jnp.left_shift(wd, 16), F32)
    hi = pltpu.bitcast(jnp.bitwise_and(wd, jnp.uint32(0xFFFF0000)), F32)
    return lo, hi


PEER_GT = 64
NSEL = PK_H * PK_TOPK
HALF = D_MODEL // 2
QROWS = HALF // LANES


SUBL = 2 * QROWS


def _peer_u_kernel(row_ref, par_ref, x_ref, tab_ref, act_ref, a_ref):
    def token(n, carry):
        xrow = x_ref[pl.ds(n, 1), :]
        rows = [xrow[:, s * LANES:(s + 1) * LANES] for s in range(SUBL)]
        xl = jnp.concatenate(rows[:QROWS] + rows[:QROWS], axis=0)
        xh = jnp.concatenate(rows[QROWS:] + rows[QROWS:], axis=0)
        base = n * NSEL
        for j in range(NSEL):
            r = pl.multiple_of(row_ref[0, 0, base + j], SUBL)
            lo, hi = _unpack_pair(tab_ref[pl.ds(r, SUBL), :])
            a_ref[j * SUBL:(j + 1) * SUBL, :] = lo * xl + hi * xh
        halves = []
        for s0 in (0, QROWS):
            acc = a_ref[pl.ds(s0, NSEL, stride=SUBL), :]
            for s in range(s0 + 1, s0 + QROWS):
                acc = acc + a_ref[pl.ds(s, NSEL, stride=SUBL), :]
            halves.append(jnp.sum(acc.T, axis=0, keepdims=True))
        act_ref[pl.ds(n, 1), :] = jnp.where(par_ref[pl.ds(n, 1), :] == 0, halves[0], halves[1])
        return carry

    lax.fori_loop(0, PEER_GT, token, 0)


def _peer_u(row3, par, hf, tab):
    nt = row3.shape[0]
    return pl.pallas_call(
        _peer_u_kernel,
        grid=(nt,),
        in_specs=[pl.BlockSpec((1, 1, PEER_GT * NSEL), lambda i: (i, 0, 0), memory_space=pltpu.SMEM),
                  pl.BlockSpec((PEER_GT, NSEL), lambda i: (i, 0)),
                  pl.BlockSpec((PEER_GT, D_MODEL), lambda i: (i, 0)),
                  _vmem_full()],
        out_specs=pl.BlockSpec((PEER_GT, NSEL), lambda i: (i, 0)),
        out_shape=jax.ShapeDtypeStruct((nt * PEER_GT, NSEL), F32),
        scratch_shapes=[pltpu.VMEM((NSEL * SUBL, LANES), F32)],
        compiler_params=_cparams(("parallel",)),
    )(row3, par, hf, tab)


def _peer_v_kernel(row_ref, c8_ref, x_ref, tab_ref, o_ref):
    low = _iota((SUBL, LANES), 0) < QROWS
    n_acc = 4

    def token(n, carry):
        base = n * NSEL
        c8 = c8_ref[n]
        alo = [jnp.zeros((SUBL, LANES), F32) for _ in range(n_acc)]
        ahi = [jnp.zeros((SUBL, LANES), F32) for _ in range(n_acc)]
        for j in range(NSEL):
            r = pl.multiple_of(row_ref[0, 0, base + j], SUBL)
            lo, hi = _unpack_pair(tab_ref[pl.ds(r, SUBL), :])
            cm = jnp.broadcast_to(c8[:, j:j + 1], (SUBL, LANES))
            alo[j % n_acc] = alo[j % n_acc] + cm * lo
            ahi[j % n_acc] = ahi[j % n_acc] + cm * hi
        tl = (alo[0] + alo[1]) + (alo[2] + alo[3])
        th = (ahi[0] + ahi[1]) + (ahi[2] + ahi[3])
        tl = tl + pltpu.roll(tl, QROWS, 0)
        th = th + pltpu.roll(th, QROWS, 0)
        t = jnp.where(low, tl, th)
        trow = jnp.concatenate([t[s:s + 1, :] for s in range(SUBL)], axis=1)
        o_ref[pl.ds(n, 1), :] = x_ref[pl.ds(n, 1), :] + trow
        return carry

    lax.fori_loop(0, PEER_GT, token, 0)


def _peer_v(row3, c8, x2, tab):
    nt = row3.shape[0]
    sm = pl.BlockSpec((1, 1, PEER_GT * NSEL), lambda i: (i, 0, 0), memory_space=pltpu.SMEM)
    xs = pl.BlockSpec((PEER_GT, D_MODEL), lambda i: (i, 0))
    return pl.pallas_call(
        _peer_v_kernel,
        grid=(nt,),
        in_specs=[sm, pl.BlockSpec((PEER_GT, SUBL, NSEL), lambda i: (i, 0, 0)), xs, _vmem_full()],
        out_specs=xs,
        out_shape=jax.ShapeDtypeStruct(x2.shape, F32),
        compiler_params=_cparams(("parallel",)),
    )(row3, c8, x2, tab)


def _pack_table(t):
    e = t.shape[0]
    b = lax.bitcast_convert_type(t.astype(BF16), jnp.uint16).astype(jnp.uint32)
    w = b[:, :HALF] | (b[:, HALF:] << 16)
    return w.reshape(e * QROWS, LANES)


def _peer(hf, x2, pw):
    n = hf.shape[0]
    nt = n // PEER_TILE
    e_t, g_t = _peer_score(hf, pw['wq'], pw['kbd'])
    tok_major = lambda a: a.reshape(nt, NSEL, PEER_TILE).transpose(0, 2, 1).reshape(n, NSEL)
    e_tm = tok_major(e_t)
    par = jnp.bitwise_and(e_tm, 1)
    row3 = (jnp.right_shift(e_tm, 1) * SUBL).reshape(n // PEER_GT, 1, PEER_GT * NSEL)
    act = _peer_u(row3, par, hf, pw['u'])
    c = tok_major(g_t) * jax.nn.gelu(act, approximate=False)
    ce = jnp.where(par == 0, c, 0.0)[:, None, :]
    co = jnp.where(par == 1, c, 0.0)[:, None, :]
    c8 = jnp.concatenate([jnp.broadcast_to(ce, (n, QROWS, NSEL)), jnp.broadcast_to(co, (n, QROWS, NSEL))], axis=1)
    return _peer_v(row3, c8, x2, pw['v'])


def _masked_softmax(s, mask):
    s = jnp.where(mask, s, -jnp.inf)
    m = jnp.max(s, axis=-1, keepdims=True)
    m = jnp.where(jnp.isfinite(m), m, 0.0)
    e = jnp.where(mask, jnp.exp(s - m), 0.0)
    den = jnp.sum(e, axis=-1, keepdims=True)
    return e / jnp.where(den > 0, den, 1.0)


def _gather_pages(pool, page_table):
    g = pool[page_table]
    return g.reshape((g.shape[0], g.shape[1] * g.shape[2]) + g.shape[3:])


def _rms(x, g):
    return x * lax.rsqrt(jnp.mean(x * x, axis=-1, keepdims=True) + EPS) * g


def _compress(rows, w_c, pe):
    b, l, h, d = rows.shape
    blk = rows.reshape(b, l // CMP_BLOCK, CMP_BLOCK, h, d) + pe[:, None, :]
    return jnp.einsum('bjihd,ide->bjhe', blk, w_c)


def _sample_attn(zs, c, page_table, lw, lam, lam_init, bsz, t_len, past_len):
    col = lambda off, w=256: zs[:, off:off + w].reshape(bsz, t_len, -1)
    hd = lambda a: a.reshape(bsz, t_len, NSA_H, HD)
    q, ks, kw = hd(col(OFF_Q)), hd(col(OFF_KS)), hd(col(OFF_KW))
    kc_r, vc_r, vs, vw = hd(col(OFF_KC)), hd(col(OFF_VC)), hd(col(OFF_VS)), hd(col(OFF_VW))
    gates = jax.nn.sigmoid(col(OFF_NG, 128)[..., :3 * NSA_H].reshape(bsz, t_len, 3, NSA_H, 1))
    qpos = past_len + jnp.arange(t_len)
    ns = jnp.asarray(NSA_SLOPES, F32)
    n_new = (past_len + t_len) // CMP_BLOCK - past_len // CMP_BLOCK

    def cmp_rows(pool, new, w_c, pe):
        zc = _compress(_gather_pages(pool, page_table), w_c, pe)
        if n_new > 0:
            zc = jnp.concatenate([zc, _compress(new[:, :n_new * CMP_BLOCK], w_c, pe)], axis=1)
        return zc

    kc = _rms(cmp_rows(c['cmp_k'], kc_r, lw['nsa_ck_w'], lw['nsa_ck_pe']), lw['nsa_kn'][0])
    vc = cmp_rows(c['cmp_v'], vc_r, lw['nsa_cv_w'], lw['nsa_cv_pe'])
    nc = kc.shape[1]
    end = jnp.arange(nc) * CMP_BLOCK + (CMP_BLOCK - 1)
    dist = (qpos[:, None] - end[None, :]).astype(F32)
    s = jnp.einsum('bthd,bjhd->bhtj', q, kc).astype(F32) - ns[:, None, None] * dist
    p_c = _masked_softmax(s, dist >= 0)
    o_c = jnp.einsum('bhtj,bjhd->bthd', p_c, vc)
    nsb = past_len // SEL_BLOCK
    kp = min(SEL_TOPK - 1, nsb)
    imp = p_c[..., :nsb * 2].reshape(bsz, NSA_H, t_len, nsb, 2).sum(-1)
    cur = qpos // SEL_BLOCK
    imp = jnp.where(jnp.arange(nsb)[None, :] < cur[:, None], imp, -1.0)
    _, idx = lax.top_k(imp, kp)
    sub_n = PAGE_SIZE // SEL_BLOCK
    bi = jnp.arange(bsz)[:, None, None, None]
    hi = jnp.arange(NSA_H)[None, :, None, None]
    phys = page_table[bi, idx // sub_n]
    sub = idx % sub_n

    def gather(pool):
        pr = pool.reshape((pool.shape[0], sub_n, SEL_BLOCK) + pool.shape[2:])
        return pr[phys, sub, :, hi, :]

    qh = q.transpose(0, 2, 1, 3)
    kg, vg = gather(c['sel_k']), gather(c['sel_v'])
    kcur, vcur = ks.transpose(0, 2, 1, 3), vs.transpose(0, 2, 1, 3)
    sl = ns.reshape(1, NSA_H, 1, 1)
    kpos = idx[..., None] * SEL_BLOCK + jnp.arange(SEL_BLOCK)
    s1 = jnp.einsum('bhtd,bhtksd->bhtks', qh, kg).astype(F32)
    s1 = s1 - sl[..., None] * (qpos[:, None, None] - kpos).astype(F32)
    m1 = jnp.broadcast_to((idx < cur[:, None])[..., None], s1.shape)
    s2 = jnp.einsum('bhtd,bhsd->bhts', qh, kcur).astype(F32)
    s2 = s2 - sl * (qpos[:, None] - qpos[None, :]).astype(F32)
    m2 = (qpos[None, :] <= qpos[:, None]) & (qpos[None, :] // SEL_BLOCK == cur[:, None])
    m2 = jnp.broadcast_to(m2, s2.shape)
    k_s = kp * SEL_BLOCK
    sa = jnp.concatenate([s1.reshape(bsz, NSA_H, t_len, k_s), s2], axis=-1)
    ma = jnp.concatenate([m1.reshape(bsz, NSA_H, t_len, k_s), m2], axis=-1)
    p = _masked_softmax(sa, ma)
    o_s = (jnp.einsum('bhtks,bhtksd->bhtd', p[..., :k_s].reshape(bsz, NSA_H, t_len, kp, SEL_BLOCK), vg)
           + jnp.einsum('bhts,bhsd->bhtd', p[..., k_s:], vcur)).transpose(0, 2, 1, 3)
    buf_k, buf_v = c['win_k'], c['win_v']
    wb = buf_k.shape[1]
    kposw = jnp.concatenate([past_len - wb + jnp.arange(wb), qpos])
    dw = qpos[:, None] - kposw[None, :]
    s = jnp.concatenate([jnp.einsum('bthd,bshd->bhts', q, buf_k),
                         jnp.einsum('bthd,bshd->bhts', q, kw)], axis=-1).astype(F32)
    s = s - ns.reshape(1, NSA_H, 1, 1) * dw.astype(F32)
    p = _masked_softmax(s, (dw >= 0) & (dw < WINDOW))
    o_w = jnp.einsum('bhts,bshd->bthd', p[..., :wb], buf_v) + jnp.einsum('bhts,bshd->bthd', p[..., wb:], vw)
    y_nsa = (gates[:, :, 0] * o_c + gates[:, :, 1] * o_s + gates[:, :, 2] * o_w).reshape(bsz * t_len, MIX_W)
    qd = col(OFF_QD).reshape(bsz, t_len, DF_H, 2, DF_DQK)
    kd = col(OFF_KD).reshape(bsz, t_len, DF_H, 2, DF_DQK)
    vd = col(OFF_VD).reshape(bsz, t_len, DF_H, DF_DV)
    kpg = _gather_pages(c['diff_k'], page_table)
    plen = kpg.shape[1]
    kpg = kpg.reshape(bsz, plen, DF_H, 2, DF_DQK)
    vpg = _gather_pages(c['diff_v'], page_table)
    kposd = jnp.concatenate([jnp.arange(plen), qpos])
    dd = (qpos[:, None] - kposd[None, :]).astype(F32)
    s = jnp.concatenate([jnp.einsum('bqhcd,bshcd->bhcqs', qd, kpg),
                         jnp.einsum('bqhcd,bshcd->bhcqs', qd, kd)], axis=-1).astype(F32)
    s = s - jnp.asarray(DF_SLOPES, F32).reshape(1, DF_H, 1, 1, 1) * dd
    p = _masked_softmax(s, dd >= 0)
    pd = p[:, :, 0] - lam * p[:, :, 1]
    o_d = jnp.einsum('bhqs,bshd->bqhd', pd[..., :plen], vpg) + jnp.einsum('bhqs,bshd->bqhd', pd[..., plen:], vd)
    y_df = (_rms(o_d, lw['df_subln']) * (1.0 - lam_init)).reshape(bsz * t_len, MIX_W)
    return y_nsa, y_df


def _block_ones(gs):
    i = np.arange(256)
    return jnp.asarray((i[:, None] // gs) == (i[None, :] // gs), BF16)


def _tile_heads(v, reps):
    return jnp.tile(v.astype(F32), reps).reshape(1, -1)


def _prep_layer(lw, l):
    m64, m32 = _block_ones(64), _block_ones(32)
    w_in = lw['w_in']
    src = np.cumsum([0, 256, 256, 256, 256, 256, 256, 256, 12, RW_P, 256, 256, 256, 256, 256, 256, 4096])
    names = ['q', 'kc', 'vc', 'ks', 'vs', 'kw', 'vw', 'ng', 'rw', 'bg', 'cg', 'xc', 'qd', 'kd', 'vd', 'gate']
    dst = dict(q=OFF_Q, kc=OFF_KC, vc=OFF_VC, ks=OFF_KS, vs=OFF_VS, kw=OFF_KW, vw=OFF_VW, ng=OFF_NG, rw=OFF_RW,
               bg=OFF_BG, cg=OFF_CG, xc=OFF_XC, qd=OFF_QD, kd=OFF_KD, vd=OFF_VD, gate=OFF_GATE)
    order = sorted(names, key=lambda nm: dst[nm])
    pieces = []
    pos = 0
    for nm in order:
        k = names.index(nm)
        assert dst[nm] == pos, (nm, dst[nm], pos)
        piece = w_in[:, int(src[k]):int(src[k + 1])]
        if nm == 'ng':
            piece = jnp.pad(piece, ((0, 0), (0, 128 - 3 * NSA_H)))
        pieces.append(piece)
        pos += piece.shape[1]
    assert pos == P_PAD
    w_pad = jnp.concatenate(pieces, axis=1).astype(BF16)
    gains = jnp.concatenate([
        _tile_heads(lw['nsa_qn'], 4) * (HD ** -0.5), _tile_heads(lw['nsa_kn'][1], 4), _tile_heads(lw['nsa_kn'][2], 4),
        _tile_heads(lw['df_qn'], 8) * (DF_DQK ** -0.5), _tile_heads(lw['df_kn'], 8),
        jnp.zeros((3, 256), F32)], axis=0)
    eye4 = jnp.eye(NSA_H, dtype=F32)
    bd = lambda w: jnp.einsum('hg,ide->ihdge', eye4, w).reshape(CMP_K, 256).astype(BF16)
    lam_init = 0.8 - 0.6 * math.exp(-0.3 * l)
    lam = (jnp.exp(jnp.sum(lw['df_lq1'] * lw['df_lk1'])) - jnp.exp(jnp.sum(lw['df_lq2'] * lw['df_lk2'])) + lam_init)
    lora = lambda w, a, b: jnp.zeros((128, 256), F32).at[a:b].set(w).astype(BF16)
    seqw = dict(mu=lw['rw_mu'].reshape(1, RW_P), w0=lw['rw_w0'].reshape(1, 256), a0=lw['rw_a0'].reshape(1, 256),
                lw=lora(lw['rw_w2'], 0, 32), la=lora(lw['rw_a2'], 32, 64), lg=lora(lw['rw_g2'], 64, 128),
                kk=lw['rw_kk'].reshape(1, 256), ka=lw['rw_ka'].reshape(1, 256),
                cw=jnp.pad(lw['conv_w'], ((0, 8 - CONV_K), (0, 0))), cb=lw['conv_b'].reshape(1, 256), m64=m64)
    mw = dict(wbr=lw['w_br'].astype(BF16), wo=lw['w_o'].astype(BF16), lnw=lw['rw_lnw'].reshape(1, 256),
              lnb=lw['rw_lnb'].reshape(1, 256), rk=lw['rw_rk'].reshape(1, 256), m64=m64,
              nx=lw['norm_x'].reshape(1, D_MODEL), wxq=lw['w_xq'].astype(BF16),
              qg=_tile_heads(lw['x_qn'], 4) * (HD ** -0.5))
    keys = lw['pk_keys'].reshape(PK_H * 2, N_KEYS, PK_DK // 2)
    eye16 = jnp.eye(PK_H * 2, dtype=F32)
    kbd = jnp.einsum('gf,gkd->gkfd', eye16, keys).reshape(PK_H * 2 * N_KEYS, PK_H * PK_DK).astype(BF16)
    pw = dict(wq=lw['pk_wq'].astype(BF16), kbd=kbd, u=_pack_table(lw['pk_u']), v=_pack_table(lw['pk_v']))
    return dict(
        g_mix=lw['norm_mix'].reshape(1, D_MODEL), w_pad=w_pad, gains=gains, m64=m64, m32=m32,
        wk_bd=bd(lw['nsa_ck_w']), wv_bd=bd(lw['nsa_cv_w']),
        pek=jnp.tile(lw['nsa_ck_pe'], (1, NSA_H)).reshape(1, CMP_K),
        pev=jnp.tile(lw['nsa_cv_pe'], (1, NSA_H)).reshape(1, CMP_K),
        gk0=_tile_heads(lw['nsa_kn'][0], 4), lam=lam.reshape(1).astype(F32), lam_init=lam_init,
        df_gain=_tile_heads(lw['df_subln'], 4) * (1.0 - lam_init), seqw=seqw, mw=mw, pw=pw,
        g_mem=lw['norm_mem'].reshape(1, D_MODEL),
        wkv=jnp.concatenate([lw['w_xk'], lw['w_xv']], axis=1).astype(BF16),
        kg=_tile_heads(lw['x_kn'], 4), wxo=lw['w_xo'].astype(BF16), nf=lw['norm_ffn'].reshape(1, D_MODEL))


def _pad_rows(x, mult):
    n = x.shape[0]
    p = (-n) % mult
    return x if p == 0 else jnp.pad(x, ((0, p),) + ((0, 0),) * (x.ndim - 1))


def _tail(x2d, z, ys, rw, mk, mv, pl_, bsz, t_len):
    y_nsa, y_scan, y_cv, y_df = ys
    r, k, v, g = rw
    x1, qx = _merge(x2d, z, y_nsa, y_scan, r, k, v, g, y_cv, y_df, pl_['mw'])
    x2, hf = _xattn(qx, mk, mv, x1, pl_['wxo'], pl_['nf'], bsz, t_len)
    n = x2.shape[0]
    out = _peer(_pad_rows(hf, PEER_TILE), _pad_rows(x2, PEER_TILE), pl_['pw'])
    return out[:n]


def _prompt_layer(x2d, mem2d, pl_, bsz, t_len):
    n = bsz * t_len
    z, zb = _proj(x2d, pl_['g_mix'], pl_['w_pad'], pl_['gains'], pl_['m64'], pl_['m32'])
    kc_raw, vc_raw = z[:, OFF_KC:OFF_KC + 256], z[:, OFF_VC:OFF_VC + 256]
    cmp4 = _cmp(kc_raw.reshape(n // SEL_BLOCK, 2 * CMP_K), vc_raw.reshape(n // SEL_BLOCK, 2 * CMP_K),
                pl_['wk_bd'], pl_['wv_bd'], pl_['pek'], pl_['pev'], pl_['m64'], pl_['gk0'])
    nsb = t_len // SEL_BLOCK
    expand = jnp.asarray(np.arange(nsb)[:, None] == (np.arange(t_len)[None, :] // SEL_BLOCK), BF16)
    y_nsa = _nsa_prompt(z, zb, cmp4, expand, bsz, t_len)
    y_df = _diff_prompt(pl_['lam'], z, zb, pl_['m64'], pl_['df_gain'], bsz, t_len)
    seq = _seq(z, jnp.zeros((bsz, 8, RW_P), F32), jnp.zeros((bsz, 8, 256), F32), pl_['seqw'], bsz, t_len)
    r, w, k, v, am, bb, g, y_cv, ulast = seq
    y_scan, s_fin = _rwkv_scan((r, w, k, v, am, bb), jnp.zeros((bsz, RW_H, RW_N, RW_N), F32), bsz, t_len)
    mk, mv = _memkv(mem2d, pl_['g_mem'], pl_['wkv'], pl_['m64'], pl_['kg'])
    mlen = mem2d.shape[0] // bsz
    mk3, mv3 = mk.reshape(bsz, mlen, 256), mv.reshape(bsz, mlen, 256)
    x_out = _tail(x2d, z, (y_nsa, y_scan, y_cv, y_df), (r, k, v, g), mk3.astype(BF16), mv3.astype(BF16),
                  pl_, bsz, t_len)
    st = lambda off: z[:, off:off + 256].reshape(bsz, t_len, NSA_H, HD)
    wb = min(WINDOW, t_len)
    shift = z[:, OFF_RW:OFF_RW + RW_P].reshape(bsz, t_len, RW_P)[:, -1]
    states = (st(OFF_KC), st(OFF_VC), st(OFF_KS), st(OFF_VS), st(OFF_KW)[:, t_len - wb:], st(OFF_VW)[:, t_len - wb:],
              st(OFF_KD), st(OFF_VD), mk3.reshape(bsz, mlen, MX_H, HD), mv3.reshape(bsz, mlen, MX_H, HD),
              s_fin, shift, ulast[:, 8 - (CONV_K - 1):])
    return x_out, states


def _sample_layer(x2d, c, page_table, lw, pl_, bsz, t_len, past_len):
    z, _ = _proj(x2d, pl_['g_mix'], pl_['w_pad'], pl_['gains'], pl_['m64'], pl_['m32'])
    y_nsa, y_df = _sample_attn(z, c, page_table, lw, pl_['lam'][0], pl_['lam_init'], bsz, t_len, past_len)
    shift0 = jnp.zeros((bsz, 8, RW_P), F32).at[:, 7].set(c['rwkv_shift'])
    conv0 = jnp.zeros((bsz, 8, 256), F32).at[:, 8 - (CONV_K - 1):].set(c['conv'])
    seq = _seq(z, shift0, conv0, pl_['seqw'], bsz, t_len)
    r, w, k, v, am, bb, g, y_cv, ulast = seq
    y_scan, s_fin = _rwkv_scan((r, w, k, v, am, bb), c['rwkv'], bsz, t_len)
    mk = c['mem_k'].reshape(bsz, -1, 256).astype(BF16)
    mv = c['mem_v'].reshape(bsz, -1, 256).astype(BF16)
    x_out = _tail(x2d, z, (y_nsa, y_scan, y_cv, y_df), (r, k, v, g), mk, mv, pl_, bsz, t_len)
    st = lambda off: z[:, off:off + 256].reshape(bsz, t_len, NSA_H, HD)
    shift = z[:, OFF_RW:OFF_RW + RW_P].reshape(bsz, t_len, RW_P)[:, -1]
    states = (st(OFF_KC), st(OFF_VC), st(OFF_KS), st(OFF_VS), st(OFF_KW), st(OFF_VW), st(OFF_KD), st(OFF_VD),
              s_fin, shift, ulast[:, 8 - (CONV_K - 1):])
    return x_out, states


def kernel(x_prompt, x_sample, mem_prompt, cache_cmp_k, cache_cmp_v, cache_sel_k, cache_sel_v, cache_win_k, cache_win_v, cache_diff_k, cache_diff_v, cache_mem_k, cache_mem_v, state_rwkv, state_rwkv_shift, state_conv, page_table, norm_mix, w_in, nsa_qn, nsa_kn, nsa_ck_w, nsa_ck_pe, nsa_cv_w, nsa_cv_pe, rw_mu, rw_w0, rw_w2, rw_a0, rw_a2, rw_g2, rw_kk, rw_ka, rw_rk, rw_lnw, rw_lnb, conv_w, conv_b, df_qn, df_kn, df_lq1, df_lk1, df_lq2, df_lk2, df_subln, w_br, w_o, norm_x, norm_mem, w_xq, w_xk, w_xv, x_qn, x_kn, w_xo, norm_ffn, pk_wq, pk_keys, pk_u, pk_v):
    bp, tp, _ = x_prompt.shape
    bs, ts, _ = x_sample.shape
    depth = w_in.shape[0]
    past_len = page_table.shape[1] * PAGE_SIZE
    weights = dict(norm_mix=norm_mix, w_in=w_in, nsa_qn=nsa_qn, nsa_kn=nsa_kn, nsa_ck_w=nsa_ck_w,
                   nsa_ck_pe=nsa_ck_pe, nsa_cv_w=nsa_cv_w, nsa_cv_pe=nsa_cv_pe, rw_mu=rw_mu, rw_w0=rw_w0,
                   rw_w2=rw_w2, rw_a0=rw_a0, rw_a2=rw_a2, rw_g2=rw_g2, rw_kk=rw_kk, rw_ka=rw_ka, rw_rk=rw_rk,
                   rw_lnw=rw_lnw, rw_lnb=rw_lnb, conv_w=conv_w, conv_b=conv_b, df_qn=df_qn, df_kn=df_kn,
                   df_lq1=df_lq1, df_lk1=df_lk1, df_lq2=df_lq2, df_lk2=df_lk2, df_subln=df_subln, w_br=w_br,
                   w_o=w_o, norm_x=norm_x, norm_mem=norm_mem, w_xq=w_xq, w_xk=w_xk, w_xv=w_xv, x_qn=x_qn,
                   x_kn=x_kn, w_xo=w_xo, norm_ffn=norm_ffn, pk_wq=pk_wq, pk_keys=pk_keys, pk_u=pk_u, pk_v=pk_v)
    xp = x_prompt.reshape(bp * tp, D_MODEL)
    xs = x_sample.reshape(bs * ts, D_MODEL)
    mem2d = mem_prompt.reshape(-1, D_MODEL)
    p_acc, s_acc = [], []
    for l in range(depth):
        lw = {k_: v_[l] for k_, v_ in weights.items()}
        pl_ = _prep_layer(lw, l)
        c = dict(cmp_k=cache_cmp_k[l], cmp_v=cache_cmp_v[l], sel_k=cache_sel_k[l], sel_v=cache_sel_v[l],
                 win_k=cache_win_k[l], win_v=cache_win_v[l], diff_k=cache_diff_k[l], diff_v=cache_diff_v[l],
                 mem_k=cache_mem_k[l], mem_v=cache_mem_v[l], rwkv=state_rwkv[l],
                 rwkv_shift=state_rwkv_shift[l], conv=state_conv[l])
        xp, ps = _prompt_layer(xp, mem2d, pl_, bp, tp)
        xs, ss = _sample_layer(xs, c, page_table, lw, pl_, bs, ts, past_len)
        p_acc.append(ps)
        s_acc.append(ss)
    p_out = [jnp.stack(a) for a in zip(*p_acc)]
    s_out = [jnp.stack(a) for a in zip(*s_acc)]
    return (xp.reshape(bp, tp, D_MODEL), xs.reshape(bs, ts, D_MODEL), *p_out, *s_out)
```

```python
import functools
import math

import jax
import jax.numpy as jnp
import numpy as np
from jax import lax
from jax.experimental import pallas as pl
from jax.experimental.pallas import tpu as pltpu

F32 = jnp.float32
BF16 = jnp.bfloat16
I32 = jnp.int32

D_MODEL = 1024
PAGE_SIZE = 128
N_BRANCH = 4
MIX_W = 256
NSA_H = 4
HD = 64
CMP_BLOCK = 32
SEL_BLOCK = 64
SEL_TOPK = 16
WINDOW = 512
RW_H = 4
RW_N = 64
DECAY_LORA = 32
AAA_LORA = 32
GATE_LORA = 64
RW_P = 896
CONV_K = 3
DF_H = 4
DF_DQK = 32
DF_DV = 64
MX_H = 4
PK_H = 8
N_KEYS = 128
PK_DK = 128
PK_TOPK = 16
EPS = 1e-6
GN_EPS = 64e-5
N_ALIBI = NSA_H + DF_H
ALIBI_SLOPES = tuple(2.0 ** (-8.0 * (i + 1) / N_ALIBI) for i in range(N_ALIBI))
NSA_SLOPES = ALIBI_SLOPES[0::2]
DF_SLOPES = ALIBI_SLOPES[1::2]

LANES = 128
VMEM_LIMIT = 56 * 1024 * 1024

OFF_GATE = 0
OFF_VD = 4096
OFF_NG = 4352
OFF_RW = 4480
OFF_Q = 5376
OFF_KS = 5632
OFF_KW = 5888
OFF_QD = 6144
OFF_KD = 6400
OFF_KC = 6656
OFF_VC = 6912
OFF_VS = 7168
OFF_VW = 7424
OFF_BG = 7680
OFF_CG = 7936
OFF_XC = 8192
P_PAD = 8448
ZB_KS, ZB_KW, ZB_KD, ZB_VS, ZB_VW, ZB_VD = range(6)


def _dot(a, b):
    return jnp.dot(a, b, preferred_element_type=F32)


def _dot_nt(a, b):
    return lax.dot_general(a, b, (((1,), (1,)), ((), ())), preferred_element_type=F32)


def _seg_sum(x, m):
    hi = x.astype(BF16)
    lo = (x - hi.astype(F32)).astype(BF16)
    return _dot(hi, m) + _dot(lo, m)


def _iota(shape, dim):
    return lax.broadcasted_iota(I32, shape, dim)


def _row_tile(n, pref):
    t = pref
    while t > 8 and n % t:
        t //= 2
    assert n % t == 0, (n, pref)
    return t


def _cparams(sem):
    return pltpu.CompilerParams(dimension_semantics=sem, vmem_limit_bytes=VMEM_LIMIT)


def _vmem_full():
    return pl.BlockSpec(memory_space=pltpu.VMEM)


_NORMED = ((OFF_Q, 64), (OFF_KS, 64), (OFF_KW, 64), (OFF_QD, 32), (OFF_KD, 32))
_ZB_SRC = ((OFF_KS, ZB_KS), (OFF_KW, ZB_KW), (OFF_KD, ZB_KD), (OFF_VS, ZB_VS), (OFF_VW, ZB_VW), (OFF_VD, ZB_VD))


def _proj_plan():
    zb_of = dict(_ZB_SRC)
    normed = dict(_NORMED)
    special = sorted(set(zb_of) | set(normed))
    plan, c0 = [], 0
    for s in special + [P_PAD]:
        while c0 < s:
            w = min(512, s - c0)
            plan.append((c0, w, 0, -1))
            c0 += w
        if s < P_PAD:
            plan.append((s, 256, normed.get(s, 0), zb_of.get(s, -1)))
            c0 = s + 256
    return plan


def _proj_kernel(x_ref, g_ref, w_ref, gains_ref, m64_ref, m32_ref, z_ref, zb_ref):
    x = x_ref[...]
    h = x * lax.rsqrt(jnp.mean(x * x, axis=-1, keepdims=True) + EPS) * g_ref[...]
    hb = h.astype(BF16)
    norm_row = {o: r for r, (o, _) in enumerate(_NORMED)}
    for c0, w, gs, slot in _proj_plan():
        z = _dot(hb, w_ref[:, c0:c0 + w])
        if gs:
            m = m64_ref[...] if gs == 64 else m32_ref[...]
            ss = _seg_sum(z * z, m)
            r = norm_row[c0]
            z = z * lax.rsqrt(ss * (1.0 / gs) + EPS) * gains_ref[r:r + 1, :]
        z_ref[:, c0:c0 + w] = z
        if slot >= 0:
            zb_ref[:, slot * 256:(slot + 1) * 256] = z.astype(BF16)


def _proj(x2d, g, w_pad, gains, m64, m32):
    n = x2d.shape[0]
    tm = _row_tile(n, 256)
    return pl.pallas_call(
        _proj_kernel,
        grid=(n // tm,),
        in_specs=[pl.BlockSpec((tm, D_MODEL), lambda i: (i, 0)),
                  pl.BlockSpec((1, D_MODEL), lambda i: (0, 0)),
                  _vmem_full(),
                  pl.BlockSpec((8, 256), lambda i: (0, 0)),
                  pl.BlockSpec((256, 256), lambda i: (0, 0)),
                  pl.BlockSpec((256, 256), lambda i: (0, 0))],
        out_specs=[pl.BlockSpec((tm, P_PAD), lambda i: (i, 0)),
                   pl.BlockSpec((tm, 1536), lambda i: (i, 0))],
        out_shape=[jax.ShapeDtypeStruct((n, P_PAD), F32), jax.ShapeDtypeStruct((n, 1536), BF16)],
        compiler_params=_cparams(("parallel",)),
    )(x2d, g, w_pad, gains, m64, m32)


CMP_K = CMP_BLOCK * 256


def _cmp_kernel(kc_ref, vc_ref, wk_ref, wv_ref, pek_ref, pev_ref, m64_ref, gk_ref,
                kce_ref, kco_ref, vce_ref, vco_ref):
    for src, w_ref, pe_ref, outs, is_k in ((kc_ref, wk_ref, pek_ref, (kce_ref, kco_ref), True),
                                           (vc_ref, wv_ref, pev_ref, (vce_ref, vco_ref), False)):
        for par in range(2):
            rows = src[:, par * CMP_K:(par + 1) * CMP_K] + pe_ref[...]
            acc = _dot(rows.astype(BF16), w_ref[...])
            if is_k:
                ss = _seg_sum(acc * acc, m64_ref[...])
                acc = acc * lax.rsqrt(ss * (1.0 / HD) + EPS) * gk_ref[...]
            outs[par][...] = acc.astype(BF16)


def _cmp(kc2, vc2, wk_bd, wv_bd, pek, pev, m64, gk):
    nr = kc2.shape[0]
    rb = _row_tile(nr, 64)
    out = jax.ShapeDtypeStruct((nr, 256), BF16)
    ospec = pl.BlockSpec((rb, 256), lambda i: (i, 0))
    ispec = pl.BlockSpec((rb, 2 * CMP_K), lambda i: (i, 0))
    par = lambda s: pl.BlockSpec(s, lambda i: (0,) * len(s))
    return pl.pallas_call(
        _cmp_kernel,
        grid=(nr // rb,),
        in_specs=[ispec, ispec, par((CMP_K, 256)), par((CMP_K, 256)), par((1, CMP_K)), par((1, CMP_K)),
                  par((256, 256)), par((1, 256))],
        out_specs=[ospec, ospec, ospec, ospec],
        out_shape=[out, out, out, out],
        compiler_params=_cparams(("parallel",)),
    )(kc2, vc2, wk_bd, wv_bd, pek, pev, m64, gk)


def _softmax_rows(s):
    m = jnp.max(s, axis=-1, keepdims=True)
    e = jnp.exp(s - m)
    return e / jnp.sum(e, axis=-1, keepdims=True)


def _nsa_kernel(q_ref, ng_ref, kce_ref, kco_ref, vce_ref, vco_ref, ks_ref, vs_ref, kw_ref, vw_ref, ex_ref,
                o_ref, *, t_len, tq, kp, band):
    i = pl.program_id(1)
    nsb = t_len // SEL_BLOCK
    q = q_ref[...]
    sig = jax.nn.sigmoid(ng_ref[...])
    lane = _iota((1, 256), 1)
    qpos = i * tq + _iota((tq, 1), 0)
    cur = jnp.right_shift(qpos, 6)
    jb = _iota((1, nsb), 1)
    kpos = _iota((1, t_len), 1)
    dist = (qpos - kpos).astype(F32)
    curf = jnp.where(jnp.right_shift(kpos, 6) == cur, jnp.where(kpos <= qpos, 1.0, 0.0), 0.0)
    de = (qpos - (jb * SEL_BLOCK + (CMP_BLOCK - 1))).astype(F32)
    do = (qpos - (jb * SEL_BLOCK + (SEL_BLOCK - 1))).astype(F32)
    past = jb < cur
    start = jnp.clip(i * tq - WINDOW, 0, t_len - band)
    start = pl.multiple_of(start, tq)
    kposw = start + _iota((1, band), 1)
    dw = qpos - kposw
    okw = jnp.where(dw >= 0, jnp.where(dw < WINDOW, 1.0, 0.0), 0.0) > 0.5
    dwf = dw.astype(F32)
    kwb = kw_ref[pl.ds(start, band), :]
    vwb = vw_ref[pl.ds(start, band), :]
    acc = jnp.zeros((tq, 256), F32)
    for h in range(NSA_H):
        hm = jnp.right_shift(lane, 6) == h
        qh = jnp.where(hm, q, 0.0).astype(BF16)
        sl = NSA_SLOPES[h]
        se = jnp.where(de >= 0, _dot_nt(qh, kce_ref[...]) - sl * de, -jnp.inf)
        so = jnp.where(do >= 0, _dot_nt(qh, kco_ref[...]) - sl * do, -jnp.inf)
        m = jnp.maximum(jnp.max(se, axis=-1, keepdims=True), jnp.max(so, axis=-1, keepdims=True))
        m = jnp.where(m > -jnp.inf, m, 0.0)
        pe = jnp.where(de >= 0, jnp.exp(se - m), 0.0)
        po = jnp.where(do >= 0, jnp.exp(so - m), 0.0)
        den = jnp.sum(pe, axis=-1, keepdims=True) + jnp.sum(po, axis=-1, keepdims=True)
        den = jnp.where(den > 0, den, 1.0)
        pe = pe / den
        po = po / den
        o_c = _dot(pe.astype(BF16), vce_ref[...]) + _dot(po.astype(BF16), vco_ref[...])
        imp = jnp.where(past, pe + po, -1.0)
        rank = jnp.zeros((tq, nsb), F32)
        for j2 in range(nsb):
            col = imp[:, j2:j2 + 1]
            before = jnp.where(jb > j2, 1.0, 0.0)
            rank = rank + jnp.where(col > imp, 1.0, jnp.where(col == imp, before, 0.0))
        member = jnp.where(past, jnp.where(rank < kp, 1.0, 0.0), 0.0)
        allowed = (_dot(member.astype(BF16), ex_ref[...]) + curf) > 0.5
        s = jnp.where(allowed, _dot_nt(qh, ks_ref[...]) - sl * dist, -jnp.inf)
        o_s = _dot(_softmax_rows(s).astype(BF16), vs_ref[...])
        s = jnp.where(okw, _dot_nt(qh, kwb) - sl * dwf, -jnp.inf)
        o_w = _dot(_softmax_rows(s).astype(BF16), vwb)
        y = (sig[:, h:h + 1] * o_c + sig[:, NSA_H + h:NSA_H + h + 1] * o_s
             + sig[:, 2 * NSA_H + h:2 * NSA_H + h + 1] * o_w)
        acc = acc + jnp.where(hm, y, 0.0)
    o_ref[...] = acc


def _nsa_prompt(z, zb, cmp4, expand, bsz, t_len):
    tq = 128
    nt = t_len // tq
    nsb = t_len // SEL_BLOCK
    kp = min(SEL_TOPK - 1, nsb)
    band = min(WINDOW + tq, t_len)
    kv = lambda c: pl.BlockSpec((t_len, 256), lambda b, i: (b, c))
    cs = pl.BlockSpec((nsb, 256), lambda b, i: (b, 0))
    return pl.pallas_call(
        functools.partial(_nsa_kernel, t_len=t_len, tq=tq, kp=kp, band=band),
        grid=(bsz, nt),
        in_specs=[pl.BlockSpec((tq, 256), lambda b, i: (b * nt + i, OFF_Q // 256)),
                  pl.BlockSpec((tq, 128), lambda b, i: (b * nt + i, OFF_NG // 128)),
                  cs, cs, cs, cs, kv(ZB_KS), kv(ZB_VS), kv(ZB_KW), kv(ZB_VW),
                  pl.BlockSpec((nsb, t_len), lambda b, i: (0, 0))],
        out_specs=pl.BlockSpec((tq, 256), lambda b, i: (b * nt + i, 0)),
        out_shape=jax.ShapeDtypeStruct((bsz * t_len, 256), F32),
        compiler_params=_cparams(("parallel", "parallel")),
    )(z, z, *cmp4, zb, zb, zb, zb, expand)


def _diff_kernel(lam_ref, q_ref, k_ref, v_ref, m64_ref, gain_ref, o_ref, *, t_len, tq):
    i = pl.program_id(1)
    lam = lam_ref[0]
    q = q_ref[...]
    lane = _iota((1, 256), 1)
    qpos = i * tq + _iota((tq, 1), 0)
    kpos = _iota((1, t_len), 1)
    causal = kpos <= qpos
    dist = (qpos - kpos).astype(F32)
    acc = jnp.zeros((tq, 256), F32)
    for h in range(DF_H):
        ps = []
        for c in range(2):
            lm = jnp.right_shift(lane, 5) == (2 * h + c)
            qh = jnp.where(lm, q, 0.0).astype(BF16)
            s = jnp.where(causal, _dot_nt(qh, k_ref[...]) - DF_SLOPES[h] * dist, -jnp.inf)
            ps.append(_softmax_rows(s))
        pd = (ps[0] - lam * ps[1]).astype(BF16)
        acc = acc + jnp.where(jnp.right_shift(lane, 6) == h, _dot(pd, v_ref[...]), 0.0)
    ss = _seg_sum(acc * acc, m64_ref[...])
    o_ref[...] = acc * lax.rsqrt(ss * (1.0 / DF_DV) + EPS) * gain_ref[...]


def _diff_prompt(lam, z, zb, m64, gain, bsz, t_len):
    tq = 128
    nt = t_len // tq
    kv = lambda c: pl.BlockSpec((t_len, 256), lambda b, i: (b, c))
    return pl.pallas_call(
        functools.partial(_diff_kernel, t_len=t_len, tq=tq),
        grid=(bsz, nt),
        in_specs=[pl.BlockSpec(memory_space=pltpu.SMEM),
                  pl.BlockSpec((tq, 256), lambda b, i: (b * nt + i, OFF_QD // 256)),
                  kv(ZB_KD), kv(ZB_VD),
                  pl.BlockSpec((256, 256), lambda b, i: (0, 0)),
                  pl.BlockSpec((1, 256), lambda b, i: (0, 0))],
        out_specs=pl.BlockSpec((tq, 256), lambda b, i: (b * nt + i, 0)),
        out_shape=jax.ShapeDtypeStruct((bsz * t_len, 256), F32),
        compiler_params=_cparams(("parallel", "parallel")),
    )(lam, z, zb, zb, m64, gain)


def _shift_rows(x, first_rows, k):
    n = x.shape[0]
    rolled = pltpu.roll(x, k, 0)
    row = _iota((n, 1), 0)
    out = rolled
    for r in range(k):
        out = jnp.where(row == r, first_rows[8 - k + r:8 - k + r + 1, :], out)
    return out


def _seq_kernel(rw_ref, rwp_ref, bg_ref, cg_ref, xc_ref, cgp_ref, xcp_ref, sh0_ref, cv0_ref,
                mu_ref, w0_ref, a0_ref, lw_ref, la_ref, lg_ref, kk_ref, ka_ref, cw_ref, cb_ref, m64_ref,
                r_ref, w_ref, k_ref, v_ref, am_ref, bb_ref, g_ref, ycv_ref, ulast_ref, *, ts):
    i = pl.program_id(1)
    x = rw_ref[...]
    first = jnp.where(i == 0, sh0_ref[0], rwp_ref[...])
    prev = _shift_rows(x, first, 1)
    xm = x + (prev - x) * mu_ref[...]
    r = xm[:, 0:256]
    k = xm[:, 256:512]
    v = xm[:, 512:768]
    lo = xm[:, 768:896]
    wlin = w0_ref[...] + _dot(jnp.tanh(lo).astype(BF16), lw_ref[...])
    w = -jax.nn.softplus(-wlin) - 0.5
    decay = jnp.exp(-jnp.exp(w))
    a = jax.nn.sigmoid(a0_ref[...] + _dot(lo.astype(BF16), la_ref[...]))
    g = _dot(jax.nn.sigmoid(lo).astype(BF16), lg_ref[...])
    kk = k * kk_ref[...]
    nrm = jnp.sqrt(_seg_sum(kk * kk, m64_ref[...]))
    kk = kk / jnp.maximum(nrm, 1e-12)
    r_ref[...] = r
    w_ref[...] = decay
    k_ref[...] = k * (1.0 + (a - 1.0) * ka_ref[...])
    v_ref[...] = v
    am_ref[...] = -kk
    bb_ref[...] = kk * a
    g_ref[...] = g
    u = cg_ref[...] * xc_ref[...]
    ufirst = jnp.where(i == 0, cv0_ref[0], cgp_ref[...] * xcp_ref[...])
    u1 = _shift_rows(u, ufirst, 1)
    u2 = _shift_rows(u, ufirst, 2)
    y = cb_ref[...] + u2 * cw_ref[0:1, :] + u1 * cw_ref[1:2, :] + u * cw_ref[2:3, :]
    ycv_ref[...] = bg_ref[...] * y
    ulast_ref[0] = u[ts - 8:ts, :]


def _seq(z, shift0, conv0, lwp, bsz, t_len):
    ts = _row_tile(t_len, 512)
    nt = t_len // ts
    n = bsz * t_len
    row = lambda w, c: pl.BlockSpec((ts, w), lambda b, i: (b * nt + i, c))
    prv = lambda w, c: pl.BlockSpec((8, w), lambda b, i: (jnp.maximum((b * nt + i) * (ts // 8) - 1, 0), c))
    par = lambda s: pl.BlockSpec(s, lambda b, i: (0,) * len(s))
    o256 = pl.BlockSpec((ts, 256), lambda b, i: (b * nt + i, 0))
    sd = jax.ShapeDtypeStruct((n, 256), F32)
    return pl.pallas_call(
        functools.partial(_seq_kernel, ts=ts),
        grid=(bsz, nt),
        in_specs=[row(RW_P, OFF_RW // RW_P), prv(RW_P, OFF_RW // RW_P),
                  row(256, OFF_BG // 256), row(256, OFF_CG // 256), row(256, OFF_XC // 256),
                  prv(256, OFF_CG // 256), prv(256, OFF_XC // 256),
                  pl.BlockSpec((1, 8, RW_P), lambda b, i: (b, 0, 0)),
                  pl.BlockSpec((1, 8, 256), lambda b, i: (b, 0, 0)),
                  par((1, RW_P)), par((1, 256)), par((1, 256)), par((128, 256)), par((128, 256)), par((128, 256)),
                  par((1, 256)), par((1, 256)), par((8, 256)), par((1, 256)), par((256, 256))],
        out_specs=[o256] * 8 + [pl.BlockSpec((1, 8, 256), lambda b, i: (b, 0, 0))],
        out_shape=[sd] * 8 + [jax.ShapeDtypeStruct((bsz, 8, 256), F32)],
        compiler_params=_cparams(("parallel", "arbitrary")),
    )(z, z, z, z, z, z, z, shift0, conv0, lwp['mu'], lwp['w0'], lwp['a0'], lwp['lw'], lwp['la'], lwp['lg'],
      lwp['kk'], lwp['ka'], lwp['cw'], lwp['cb'], lwp['m64'])


def _scan_kernel(s0_ref, am_ref, w_ref, bb_ref, k_ref, r_ref, v_ref, y_ref, s_ref, *, tc, ng, ih):
    @pl.when(pl.program_id(0) == 0)
    def _():
        s_ref[...] = s0_ref[...]

    def step(t, carry):
        def group(g, c2):
            h = g // ih
            hs = pl.multiple_of(h * RW_N, RW_N)
            st = s_ref[g]
            a_c = am_ref[t, pl.ds(hs, RW_N), :]
            sa = jnp.sum(st * a_c, axis=0, keepdims=True)
            vrow = v_ref[t, pl.ds(g, 1), :]
            st = (st * w_ref[t, pl.ds(hs, RW_N), :] + sa * bb_ref[t, pl.ds(hs, RW_N), :]
                  + vrow * k_ref[t, pl.ds(hs, RW_N), :])
            s_ref[g] = st
            y_ref[t, pl.ds(g, 1), :] = jnp.sum(st * r_ref[t, pl.ds(hs, RW_N), :], axis=0, keepdims=True)
            return c2

        return lax.fori_loop(0, ng, group, carry, unroll=4)

    lax.fori_loop(0, tc, step, 0)


def _scan(s0, am, w, bb, k, r, v, t_len, ih):
    ng = RW_H * ih
    tc = _row_tile(t_len, 16)
    col = pl.BlockSpec((tc, 256, LANES), lambda i: (i, 0, 0))
    rowv = pl.BlockSpec((tc, ng, LANES), lambda i: (i, 0, 0))
    st = pl.BlockSpec((ng, RW_N, LANES), lambda i: (0, 0, 0))
    return pl.pallas_call(
        functools.partial(_scan_kernel, tc=tc, ng=ng, ih=ih),
        grid=(t_len // tc,),
        in_specs=[st, col, col, col, col, col, rowv],
        out_specs=[rowv, st],
        out_shape=[jax.ShapeDtypeStruct((t_len, ng, LANES), F32), jax.ShapeDtypeStruct((ng, RW_N, LANES), F32)],
        compiler_params=_cparams(("arbitrary",)),
    )(s0, am, w, bb, k, r, v)


def _to_cols(x, bsz, t_len, il):
    x = x.reshape(bsz, t_len, 256).transpose(1, 2, 0)
    return jnp.tile(x, (1, 1, il))


def _to_rows(x, bsz, t_len, il, ih):
    x = x.reshape(bsz, t_len, RW_H, ih, il).transpose(1, 2, 3, 4, 0)
    return x.reshape(t_len, RW_H * ih, il * bsz)


def _from_rows(y, bsz, t_len, il, ih):
    y = y.reshape(t_len, RW_H, ih, il, bsz).transpose(4, 0, 1, 2, 3)
    return y.reshape(bsz * t_len, 256)


def _state_in(s, bsz, il, ih):
    s = s.reshape(bsz, RW_H, ih, il, RW_N).transpose(1, 2, 4, 3, 0)
    return s.reshape(RW_H * ih, RW_N, il * bsz)


def _state_out(s, bsz, il, ih):
    s = s.reshape(RW_H, ih, RW_N, il, bsz).transpose(4, 0, 1, 3, 2)
    return s.reshape(bsz, RW_H, RW_N, RW_N)


def _rwkv_scan(seq_out, s0, bsz, t_len):
    r, w, k, v, am, bb = seq_out
    il = LANES // bsz
    ih = RW_N // il
    cols = [_to_cols(a, bsz, t_len, il) for a in (am, w, bb, k, r)]
    y, s_fin = _scan(_state_in(s0, bsz, il, ih), *cols, _to_rows(v, bsz, t_len, il, ih), t_len, ih)
    return _from_rows(y, bsz, t_len, il, ih), _state_out(s_fin, bsz, il, ih)


def _merge_kernel(x_ref, gate_ref, ynsa_ref, yrw_ref, r_ref, k_ref, v_ref, g_ref, ycv_ref, ydf_ref,
                  wbr_ref, wo_ref, lnw_ref, lnb_ref, rk_ref, m64_ref, nx_ref, wxq_ref, qg_ref,
                  x1_ref, qx_ref):
    m64 = m64_ref[...]
    y = yrw_ref[...]
    mu = _seg_sum(y, m64) * (1.0 / RW_N)
    d = y - mu
    var = _seg_sum(d * d, m64) * (1.0 / RW_N)
    yn = d * lax.rsqrt(var + GN_EPS) * lnw_ref[...] + lnb_ref[...]
    v = v_ref[...]
    bonus = _seg_sum(r_ref[...] * k_ref[...] * rk_ref[...], m64) * v
    y_rw = (yn + bonus) * g_ref[...]
    ys = (ynsa_ref[...], y_rw, ycv_ref[...], ydf_ref[...])
    zsum = None
    for b in range(N_BRANCH):
        t = jax.nn.sigmoid(gate_ref[:, b * D_MODEL:(b + 1) * D_MODEL]) * _dot(ys[b].astype(BF16), wbr_ref[b])
        zsum = t if zsum is None else zsum + t
    x1 = x_ref[...] + _dot(zsum.astype(BF16), wo_ref[...])
    x1_ref[...] = x1
    hx = x1 * lax.rsqrt(jnp.mean(x1 * x1, axis=-1, keepdims=True) + EPS) * nx_ref[...]
    qx = _dot(hx.astype(BF16), wxq_ref[...])
    ss = _seg_sum(qx * qx, m64)
    qx_ref[...] = qx * lax.rsqrt(ss * (1.0 / HD) + EPS) * qg_ref[...]


def _merge(x2d, z, ynsa, yrw, r, k, v, g, ycv, ydf, mw):
    n = x2d.shape[0]
    tm = _row_tile(n, 256)
    row = lambda w, c=0: pl.BlockSpec((tm, w), lambda i: (i, c))
    par = lambda s: pl.BlockSpec(s, lambda i: (0,) * len(s))
    return pl.pallas_call(
        _merge_kernel,
        grid=(n // tm,),
        in_specs=[row(D_MODEL), row(4 * D_MODEL, OFF_GATE)] + [row(256)] * 8
                 + [par((4, 256, D_MODEL)), par((D_MODEL, D_MODEL)), par((1, 256)), par((1, 256)), par((1, 256)),
                    par((256, 256)), par((1, D_MODEL)), par((D_MODEL, 256)), par((1, 256))],
        out_specs=[row(D_MODEL), row(256)],
        out_shape=[jax.ShapeDtypeStruct((n, D_MODEL), F32), jax.ShapeDtypeStruct((n, 256), F32)],
        compiler_params=_cparams(("parallel",)),
    )(x2d, z, ynsa, yrw, r, k, v, g, ycv, ydf, mw['wbr'], mw['wo'], mw['lnw'], mw['lnb'], mw['rk'], mw['m64'],
      mw['nx'], mw['wxq'], mw['qg'])


def _memkv_kernel(x_ref, g_ref, w_ref, m64_ref, kg_ref, k_ref, v_ref):
    x = x_ref[...]
    h = (x * lax.rsqrt(jnp.mean(x * x, axis=-1, keepdims=True) + EPS) * g_ref[...]).astype(BF16)
    k = _dot(h, w_ref[:, 0:256])
    ss = _seg_sum(k * k, m64_ref[...])
    k_ref[...] = k * lax.rsqrt(ss * (1.0 / HD) + EPS) * kg_ref[...]
    v_ref[...] = _dot(h, w_ref[:, 256:512])


def _memkv(mem2d, g, wkv, m64, kg):
    n = mem2d.shape[0]
    tm = _row_tile(n, 256)
    par = lambda s: pl.BlockSpec(s, lambda i: (0,) * len(s))
    row = lambda w: pl.BlockSpec((tm, w), lambda i: (i, 0))
    sd = jax.ShapeDtypeStruct((n, 256), F32)
    return pl.pallas_call(
        _memkv_kernel,
        grid=(n // tm,),
        in_specs=[row(D_MODEL), par((1, D_MODEL)), par((D_MODEL, 512)), par((256, 256)), par((1, 256))],
        out_specs=[row(256), row(256)],
        out_shape=[sd, sd],
        compiler_params=_cparams(("parallel",)),
    )(mem2d, g, wkv, m64, kg)


def _xattn_kernel(qx_ref, mk_ref, mv_ref, x1_ref, wxo_ref, nf_ref, x2_ref, hf_ref):
    q = qx_ref[...]
    lane = _iota((1, 256), 1)
    mk = mk_ref[0]
    mv = mv_ref[0]
    acc = jnp.zeros(q.shape, F32)
    for h in range(MX_H):
        hm = jnp.right_shift(lane, 6) == h
        s = _dot_nt(jnp.where(hm, q, 0.0).astype(BF16), mk)
        acc = acc + jnp.where(hm, _dot(_softmax_rows(s).astype(BF16), mv), 0.0)
    x2 = x1_ref[...] + _dot(acc.astype(BF16), wxo_ref[...])
    x2_ref[...] = x2
    hf_ref[...] = x2 * lax.rsqrt(jnp.mean(x2 * x2, axis=-1, keepdims=True) + EPS) * nf_ref[...]


def _xattn(qx, mk, mv, x1, wxo, nf, bsz, t_len):
    tq = _row_tile(t_len, 256)
    nt = t_len // tq
    mlen = mk.shape[1]
    row = lambda w: pl.BlockSpec((tq, w), lambda b, i: (b * nt + i, 0))
    mem = pl.BlockSpec((1, mlen, 256), lambda b, i: (b, 0, 0))
    sd = jax.ShapeDtypeStruct((bsz * t_len, D_MODEL), F32)
    return pl.pallas_call(
        _xattn_kernel,
        grid=(bsz, nt),
        in_specs=[row(256), mem, mem, row(D_MODEL),
                  pl.BlockSpec((256, D_MODEL), lambda b, i: (0, 0)),
                  pl.BlockSpec((1, D_MODEL), lambda b, i: (0, 0))],
        out_specs=[row(D_MODEL), row(D_MODEL)],
        out_shape=[sd, sd],
        compiler_params=_cparams(("parallel", "parallel")),
    )(qx, mk, mv, x1, wxo, nf)


PEER_TILE = 128


def _topk_cols(s, ids, n_take, payload=None):
    big = PK_TOPK * PK_TOPK + N_KEYS
    vals, picks = [], []
    for _ in range(n_take):
        m = jnp.max(s, axis=0, keepdims=True)
        ix = jnp.min(jnp.where(s == m, ids, big), axis=0, keepdims=True)
        hit = ids == ix
        vals.append(m)
        picks.append(ix if payload is None else jnp.max(jnp.where(hit, payload, -1), axis=0, keepdims=True))
        s = jnp.where(hit, -jnp.inf, s)
    return jnp.concatenate(vals, axis=0), jnp.concatenate(picks, axis=0)


def _peer_score_kernel(hf_ref, wq_ref, kbd_ref, e_ref, g_ref):
    q = _dot(hf_ref[...].astype(BF16), wq_ref[...])
    st = _dot_nt(kbd_ref[...], q.astype(BF16))
    kid = _iota((N_KEYS, PEER_TILE), 0)
    half = PK_TOPK // 2
    cids, cmask = [], []
    for a in range(half):
        nb = PK_TOPK // (a + 1)
        rows = -(-nb // 8) * 8
        b_io = _iota((rows, PEER_TILE), 0)
        cids.append(a * PK_TOPK + b_io)
        cmask.append(b_io < nb)
    cids.append((half + _iota((half, PEER_TILE), 0)) * PK_TOPK)
    cid = jnp.concatenate(cids, axis=0)
    for h in range(PK_H):
        sv0, si0 = _topk_cols(st[(2 * h) * N_KEYS:(2 * h + 1) * N_KEYS, :], kid, PK_TOPK)
        sv1, si1 = _topk_cols(st[(2 * h + 1) * N_KEYS:(2 * h + 2) * N_KEYS, :], kid, PK_TOPK)
        cparts, eparts = [], []
        for a in range(half):
            rows = cids[a].shape[0]
            cparts.append(jnp.where(cmask[a], sv0[a:a + 1, :] + sv1[0:rows, :], -jnp.inf))
            eparts.append(si0[a:a + 1, :] * N_KEYS + si1[0:rows, :])
        cparts.append(sv0[half:, :] + sv1[0:1, :])
        eparts.append(si0[half:, :] * N_KEYS + si1[0:1, :])
        cand = jnp.concatenate(cparts, axis=0)
        ecand = jnp.concatenate(eparts, axis=0)
        cv, ev = _topk_cols(cand, cid, PK_TOPK, payload=ecand)
        ex = jnp.exp(cv - cv[0:1, :])
        g_ref[h * PK_TOPK:(h + 1) * PK_TOPK, :] = ex / jnp.sum(ex, axis=0, keepdims=True)
        e_ref[h * PK_TOPK:(h + 1) * PK_TOPK, :] = ev


def _peer_score(hf, wq, kbd):
    n = hf.shape[0]
    nt = n // PEER_TILE
    nsel = PK_H * PK_TOPK
    out = pl.BlockSpec((nsel, PEER_TILE), lambda i: (i, 0))
    return pl.pallas_call(
        _peer_score_kernel,
        grid=(nt,),
        in_specs=[pl.BlockSpec((PEER_TILE, D_MODEL), lambda i: (i, 0)),
                  pl.BlockSpec((D_MODEL, PK_H * PK_DK), lambda i: (0, 0)),
                  pl.BlockSpec((PK_H * 2 * N_KEYS, PK_H * PK_DK), lambda i: (0, 0))],
        out_specs=[out, out],
        out_shape=[jax.ShapeDtypeStruct((nt * nsel, PEER_TILE), I32),
                   jax.ShapeDtypeStruct((nt * nsel, PEER_TILE), F32)],
        compiler_params=_cparams(("parallel",)),
    )(hf, wq, kbd)


def _unpack_pair(wd):
    lo = pltpu.bitcast(jnp.left_shift(wd, 16), F32)
    hi = pltpu.bitcast(jnp.bitwise_and(wd, jnp.uint32(0xFFFF0000)), F32)
    return lo, hi


PEER_GT = 64
NSEL = PK_H * PK_TOPK
HALF = D_MODEL // 2
QROWS = HALF // LANES


SUBL = 2 * QROWS


def _peer_u_kernel(row_ref, par_ref, x_ref, tab_ref, act_ref, a_ref):
    def token(n, carry):
        xrow = x_ref[pl.ds(n, 1), :]
        rows = [xrow[:, s * LANES:(s + 1) * LANES] for s in range(SUBL)]
        xl = jnp.concatenate(rows[:QROWS] + rows[:QROWS], axis=0)
        xh = jnp.concatenate(rows[QROWS:] + rows[QROWS:], axis=0)
        base = n * NSEL
        for j in range(NSEL):
            r = pl.multiple_of(row_ref[0, 0, base + j], SUBL)
            lo, hi = _unpack_pair(tab_ref[pl.ds(r, SUBL), :])
            a_ref[j * SUBL:(j + 1) * SUBL, :] = lo * xl + hi * xh
        halves = []
        for s0 in (0, QROWS):
            acc = a_ref[pl.ds(s0, NSEL, stride=SUBL), :]
            for s in range(s0 + 1, s0 + QROWS):
                acc = acc + a_ref[pl.ds(s, NSEL, stride=SUBL), :]
            halves.append(jnp.sum(acc.T, axis=0, keepdims=True))
        act_ref[pl.ds(n, 1), :] = jnp.where(par_ref[pl.ds(n, 1), :] == 0, halves[0], halves[1])
        return carry

    lax.fori_loop(0, PEER_GT, token, 0)


def _peer_u(row3, par, hf, tab):
    nt = row3.shape[0]
    return pl.pallas_call(
        _peer_u_kernel,
        grid=(nt,),
        in_specs=[pl.BlockSpec((1, 1, PEER_GT * NSEL), lambda i: (i, 0, 0), memory_space=pltpu.SMEM),
                  pl.BlockSpec((PEER_GT, NSEL), lambda i: (i, 0)),
                  pl.BlockSpec((PEER_GT, D_MODEL), lambda i: (i, 0)),
                  _vmem_full()],
        out_specs=pl.BlockSpec((PEER_GT, NSEL), lambda i: (i, 0)),
        out_shape=jax.ShapeDtypeStruct((nt * PEER_GT, NSEL), F32),
        scratch_shapes=[pltpu.VMEM((NSEL * SUBL, LANES), F32)],
        compiler_params=_cparams(("parallel",)),
    )(row3, par, hf, tab)


def _peer_v_kernel(row_ref, c8_ref, x_ref, tab_ref, o_ref):
    low = _iota((SUBL, LANES), 0) < QROWS
    n_acc = 4

    def token(n, carry):
        base = n * NSEL
        c8 = c8_ref[n]
        alo = [jnp.zeros((SUBL, LANES), F32) for _ in range(n_acc)]
        ahi = [jnp.zeros((SUBL, LANES), F32) for _ in range(n_acc)]
        for j in range(NSEL):
            r = pl.multiple_of(row_ref[0, 0, base + j], SUBL)
            lo, hi = _unpack_pair(tab_ref[pl.ds(r, SUBL), :])
            cm = jnp.broadcast_to(c8[:, j:j + 1], (SUBL, LANES))
            alo[j % n_acc] = alo[j % n_acc] + cm * lo
            ahi[j % n_acc] = ahi[j % n_acc] + cm * hi
        tl = (alo[0] + alo[1]) + (alo[2] + alo[3])
        th = (ahi[0] + ahi[1]) + (ahi[2] + ahi[3])
        tl = tl + pltpu.roll(tl, QROWS, 0)
        th = th + pltpu.roll(th, QROWS, 0)
        t = jnp.where(low, tl, th)
        trow = jnp.concatenate([t[s:s + 1, :] for s in range(SUBL)], axis=1)
        o_ref[pl.ds(n, 1), :] = x_ref[pl.ds(n, 1), :] + trow
        return carry

    lax.fori_loop(0, PEER_GT, token, 0)


def _peer_v(row3, c8, x2, tab):
    nt = row3.shape[0]
    sm = pl.BlockSpec((1, 1, PEER_GT * NSEL), lambda i: (i, 0, 0), memory_space=pltpu.SMEM)
    xs = pl.BlockSpec((PEER_GT, D_MODEL), lambda i: (i, 0))
    return pl.pallas_call(
        _peer_v_kernel,
        grid=(nt,),
        in_specs=[sm, pl.BlockSpec((PEER_GT, SUBL, NSEL), lambda i: (i, 0, 0)), xs, _vmem_full()],
        out_specs=xs,
        out_shape=jax.ShapeDtypeStruct(x2.shape, F32),
        compiler_params=_cparams(("parallel",)),
    )(row3, c8, x2, tab)


def _pack_table(t):
    e = t.shape[0]
    b = lax.bitcast_convert_type(t.astype(BF16), jnp.uint16).astype(jnp.uint32)
    w = b[:, :HALF] | (b[:, HALF:] << 16)
    return w.reshape(e * QROWS, LANES)


def _peer(hf, x2, pw):
    n = hf.shape[0]
    nt = n // PEER_TILE
    e_t, g_t = _peer_score(hf, pw['wq'], pw['kbd'])
    tok_major = lambda a: a.reshape(nt, NSEL, PEER_TILE).transpose(0, 2, 1).reshape(n, NSEL)
    e_tm = tok_major(e_t)
    par = jnp.bitwise_and(e_tm, 1)
    row3 = (jnp.right_shift(e_tm, 1) * SUBL).reshape(n // PEER_GT, 1, PEER_GT * NSEL)
    act = _peer_u(row3, par, hf, pw['u'])
    c = tok_major(g_t) * jax.nn.gelu(act, approximate=False)
    ce = jnp.where(par == 0, c, 0.0)[:, None, :]
    co = jnp.where(par == 1, c, 0.0)[:, None, :]
    c8 = jnp.concatenate([jnp.broadcast_to(ce, (n, QROWS, NSEL)), jnp.broadcast_to(co, (n, QROWS, NSEL))], axis=1)
    return _peer_v(row3, c8, x2, pw['v'])


def _masked_softmax(s, mask):
    s = jnp.where(mask, s, -jnp.inf)
    m = jnp.max(s, axis=-1, keepdims=True)
    m = jnp.where(jnp.isfinite(m), m, 0.0)
    e = jnp.where(mask, jnp.exp(s - m), 0.0)
    den = jnp.sum(e, axis=-1, keepdims=True)
    return e / jnp.where(den > 0, den, 1.0)


def _gather_pages(pool, page_table):
    g = pool[page_table]
    return g.reshape((g.shape[0], g.shape[1] * g.shape[2]) + g.shape[3:])


def _rms(x, g):
    return x * lax.rsqrt(jnp.mean(x * x, axis=-1, keepdims=True) + EPS) * g


def _compress(rows, w_c, pe):
    b, l, h, d = rows.shape
    blk = rows.reshape(b, l // CMP_BLOCK, CMP_BLOCK, h, d) + pe[:, None, :]
    return jnp.einsum('bjihd,ide->bjhe', blk, w_c)


DEC_PAGES = 8


def _online_update(m_ref, l_ref, a_ref, s, v):
    m_old = m_ref[...]
    m_new = jnp.maximum(m_old, jnp.max(s, axis=-1, keepdims=True))
    m_safe = jnp.where(m_new > -jnp.inf, m_new, 0.0)
    p = jnp.exp(s - m_safe)
    alpha = jnp.exp(m_old - m_safe)
    l_ref[...] = alpha * l_ref[...] + jnp.sum(p, axis=-1, keepdims=True)
    a_ref[...] = alpha * a_ref[...] + _dot(p.astype(BF16), v)
    m_ref[...] = m_new


def _decode_kernel(pt_ref, lam_ref, q_ref, qd_ref, ks_ref, vs_ref, kw_ref, vw_ref, kd_ref, vd_ref, oc_ref, ng_ref,
                   mem_ref, wk_ref, wv_ref, ex_ref, m64_ref, dgain_ref, *rest, ch, ts, past_len, nch):
    pages = rest[:4 * ch]
    ynsa_ref, ydf_ref = rest[4 * ch:4 * ch + 2]
    ms_ref, ls_ref, as_ref, md_ref, ld_ref, ad_ref = rest[4 * ch + 2:]
    c = pl.program_id(1)
    rs = NSA_H * ts
    rd = 2 * DF_H * ts

    @pl.when(c == 0)
    def _():
        ms_ref[...] = jnp.full(ms_ref.shape, -jnp.inf, F32)
        md_ref[...] = jnp.full(md_ref.shape, -jnp.inf, F32)
        ls_ref[...] = jnp.zeros(ls_ref.shape, F32)
        ld_ref[...] = jnp.zeros(ld_ref.shape, F32)
        as_ref[...] = jnp.zeros(as_ref.shape, F32)
        ad_ref[...] = jnp.zeros(ad_ref.shape, F32)

    lane = _iota((1, 256), 1)
    q = q_ref[...]
    qd = qd_ref[...]
    qbd = jnp.concatenate([jnp.where(jnp.right_shift(lane, 6) == h, q, 0.0) for h in range(NSA_H)],
                          axis=0).astype(BF16)
    qdbd = jnp.concatenate([jnp.where(jnp.right_shift(lane, 5) == 2 * h + cc, qd, 0.0)
                            for cc in range(2) for h in range(DF_H)], axis=0).astype(BF16)
    row_s = _iota((rs, 1), 0)
    row_d = _iota((rd, 1), 0)
    tok_s, tok_d = row_s, row_d
    sl_s = jnp.full((rs, 1), NSA_SLOPES[0], F32)
    sl_d = jnp.full((rd, 1), DF_SLOPES[0], F32)
    for g in range(1, rs // ts):
        tok_s = jnp.where(row_s >= g * ts, row_s - g * ts, tok_s)
        sl_s = jnp.where(row_s >= g * ts, NSA_SLOPES[g % NSA_H], sl_s)
    for g in range(1, rd // ts):
        tok_d = jnp.where(row_d >= g * ts, row_d - g * ts, tok_d)
        sl_d = jnp.where(row_d >= g * ts, DF_SLOPES[g % DF_H], sl_d)
    qpos_s = past_len + tok_s
    qpos_d = past_len + tok_d
    nk = ch * PAGE_SIZE
    kpos = c * nk + _iota((1, nk), 1)
    cat = lambda refs: jnp.concatenate([r[...] for r in refs], axis=0).astype(BF16)
    ksel, vsel = cat(pages[0:ch]), cat(pages[ch:2 * ch])
    kdf, vdf = cat(pages[2 * ch:3 * ch]), cat(pages[3 * ch:4 * ch])
    allowed = _dot(mem_ref[...].astype(BF16), ex_ref[...]) > 0.5
    s = jnp.where(allowed, _dot_nt(qbd, ksel) - sl_s * (qpos_s - kpos).astype(F32), -jnp.inf)
    _online_update(ms_ref, ls_ref, as_ref, s, vsel)
    s = _dot_nt(qdbd, kdf) - sl_d * (qpos_d - kpos).astype(F32)
    _online_update(md_ref, ld_ref, ad_ref, s, vdf)

    @pl.when(c == nch - 1)
    def _():
        pad = lambda ref: jnp.concatenate([ref[...], jnp.zeros((PAGE_SIZE - ts, 256), F32)], axis=0).astype(BF16)
        kposn = past_len + _iota((1, PAGE_SIZE), 1)
        dn_s = qpos_s - kposn
        dn_d = qpos_d - kposn
        ok = jnp.where(dn_s >= 0, jnp.where(jnp.right_shift(kposn, 6) == jnp.right_shift(qpos_s, 6), 1.0, 0.0),
                       0.0) > 0.5
        s_n = jnp.where(ok, _dot_nt(qbd, pad(ks_ref)) - sl_s * dn_s.astype(F32), -jnp.inf)
        _online_update(ms_ref, ls_ref, as_ref, s_n, pad(vs_ref))
        o_s = as_ref[...] / ls_ref[...]
        s_n = jnp.where(dn_d >= 0, _dot_nt(qdbd, pad(kd_ref)) - sl_d * dn_d.astype(F32), -jnp.inf)
        _online_update(md_ref, ld_ref, ad_ref, s_n, pad(vd_ref))
        od = ad_ref[...] / ld_ref[...]
        o_d = od[:rs, :] - lam_ref[0] * od[rs:, :]
        wb = wk_ref.shape[0]
        dw = qpos_s - (past_len - wb + _iota((1, wb), 1))
        okw = jnp.where(dw >= 0, jnp.where(dw < WINDOW, 1.0, 0.0), 0.0) > 0.5
        s1 = jnp.where(okw, _dot_nt(qbd, wk_ref[...].astype(BF16)) - sl_s * dw.astype(F32), -jnp.inf)
        okn = jnp.where(dn_s >= 0, jnp.where(dn_s < WINDOW, 1.0, 0.0), 0.0) > 0.5
        s2 = jnp.where(okn, _dot_nt(qbd, pad(kw_ref)) - sl_s * dn_s.astype(F32), -jnp.inf)
        m = jnp.maximum(jnp.max(s1, axis=-1, keepdims=True), jnp.max(s2, axis=-1, keepdims=True))
        e1 = jnp.exp(s1 - m)
        e2 = jnp.exp(s2 - m)
        den = jnp.sum(e1, axis=-1, keepdims=True) + jnp.sum(e2, axis=-1, keepdims=True)
        o_w = (_dot(e1.astype(BF16), wv_ref[...].astype(BF16)) + _dot(e2.astype(BF16), pad(vw_ref))) / den
        sig = jax.nn.sigmoid(ng_ref[...])
        y = jnp.zeros((ts, 256), F32)
        od_t = jnp.zeros((ts, 256), F32)
        for h in range(NSA_H):
            hm = jnp.right_shift(lane, 6) == h
            rows = slice(h * ts, (h + 1) * ts)
            yh = (sig[:, h:h + 1] * oc_ref[...] + sig[:, NSA_H + h:NSA_H + h + 1] * o_s[rows, :]
                  + sig[:, 2 * NSA_H + h:2 * NSA_H + h + 1] * o_w[rows, :])
            y = y + jnp.where(hm, yh, 0.0)
            od_t = od_t + jnp.where(hm, o_d[rows, :], 0.0)
        ynsa_ref[...] = y
        ss = _seg_sum(od_t * od_t, m64_ref[...])
        ydf_ref[...] = od_t * lax.rsqrt(ss * (1.0 / DF_DV) + EPS) * dgain_ref[...]


def _sample_decode(zs, o_c, member, c, page_table, dec, bsz, ts, past_len):
    n_pages = page_table.shape[1]
    ch = math.gcd(DEC_PAGES, n_pages)
    nch = n_pages // ch
    sbc = ch * (PAGE_SIZE // SEL_BLOCK)
    rs = NSA_H * ts
    layer = dec['l']
    mem4 = member.reshape(bsz, rs, nch, sbc).transpose(0, 2, 1, 3)
    expand = jnp.asarray(np.arange(sbc)[:, None] == (np.arange(ch * PAGE_SIZE)[None, :] // SEL_BLOCK), BF16)
    pool = lambda a: a.reshape(a.shape[0], a.shape[1], PAGE_SIZE, 256)
    wbuf = lambda a: a.reshape(a.shape[0], a.shape[1], a.shape[2], 256)
    zcol = lambda off, w=256: pl.BlockSpec((ts, w), lambda b, cc, pt: (b, off // w))
    par = lambda s: pl.BlockSpec(s, lambda b, cc, pt: (0,) * len(s))
    wspec = pl.BlockSpec((None, None, dec['win_k'].shape[2], 256), lambda b, cc, pt: (layer, b, 0, 0))
    page = lambda k: pl.BlockSpec((None, None, PAGE_SIZE, 256),
                                  lambda b, cc, pt: (layer, pt[b * n_pages + cc * ch + k], 0, 0))
    in_specs = [pl.BlockSpec(memory_space=pltpu.SMEM),
                zcol(OFF_Q), zcol(OFF_QD), zcol(OFF_KS), zcol(OFF_VS), zcol(OFF_KW), zcol(OFF_VW), zcol(OFF_KD),
                zcol(OFF_VD), pl.BlockSpec((ts, 256), lambda b, cc, pt: (b, 0)), zcol(OFF_NG, 128),
                pl.BlockSpec((None, None, rs, sbc), lambda b, cc, pt: (b, cc, 0, 0)),
                wspec, wspec, par((sbc, ch * PAGE_SIZE)), par((256, 256)), par((1, 256))]
    in_specs += [page(k) for _ in range(4) for k in range(ch)]
    ospec = pl.BlockSpec((ts, 256), lambda b, cc, pt: (b, 0))
    pools = [pool(dec[nm]) for nm in ('sel_k', 'sel_v', 'diff_k', 'diff_v')]
    page_args = [p for p in pools for _ in range(ch)]
    sd = jax.ShapeDtypeStruct((bsz * ts, 256), F32)
    grid_spec = pltpu.PrefetchScalarGridSpec(
        num_scalar_prefetch=1, grid=(bsz, nch), in_specs=in_specs, out_specs=[ospec, ospec],
        scratch_shapes=[pltpu.VMEM((rs, 1), F32), pltpu.VMEM((rs, 1), F32), pltpu.VMEM((rs, 256), F32),
                        pltpu.VMEM((2 * rs, 1), F32), pltpu.VMEM((2 * rs, 1), F32), pltpu.VMEM((2 * rs, 256), F32)])
    return pl.pallas_call(
        functools.partial(_decode_kernel, ch=ch, ts=ts, past_len=past_len, nch=nch),
        grid_spec=grid_spec,
        out_shape=[sd, sd],
        compiler_params=_cparams(("parallel", "arbitrary")),
    )(page_table.reshape(-1), dec['lam'], zs, zs, zs, zs, zs, zs, zs, zs, o_c, zs, mem4,
      wbuf(dec['win_k']), wbuf(dec['win_v']), expand, dec['m64'], dec['df_gain'], *page_args)


def _sample_attn(zs, c, page_table, lw, dec, bsz, t_len, past_len):
    col = lambda off, w=256: zs[:, off:off + w].reshape(bsz, t_len, -1)
    hd = lambda a: a.reshape(bsz, t_len, NSA_H, HD)
    q = hd(col(OFF_Q))
    kc_r, vc_r = hd(col(OFF_KC)), hd(col(OFF_VC))
    qpos = past_len + jnp.arange(t_len)
    ns = jnp.asarray(NSA_SLOPES, F32)
    n_new = (past_len + t_len) // CMP_BLOCK - past_len // CMP_BLOCK

    def cmp_rows(pool, new, w_c, pe):
        zc = _compress(_gather_pages(pool, page_table), w_c, pe)
        if n_new > 0:
            zc = jnp.concatenate([zc, _compress(new[:, :n_new * CMP_BLOCK], w_c, pe)], axis=1)
        return zc

    kc = _rms(cmp_rows(c['cmp_k'], kc_r, lw['nsa_ck_w'], lw['nsa_ck_pe']), lw['nsa_kn'][0])
    vc = cmp_rows(c['cmp_v'], vc_r, lw['nsa_cv_w'], lw['nsa_cv_pe'])
    nc = kc.shape[1]
    end = jnp.arange(nc) * CMP_BLOCK + (CMP_BLOCK - 1)
    dist = (qpos[:, None] - end[None, :]).astype(F32)
    s = jnp.einsum('bthd,bjhd->bhtj', q, kc).astype(F32) - ns[:, None, None] * dist
    p_c = _masked_softmax(s, dist >= 0)
    o_c = jnp.einsum('bhtj,bjhd->bthd', p_c, vc)
    nsb = past_len // SEL_BLOCK
    kp = min(SEL_TOPK - 1, nsb)
    imp = p_c[..., :nsb * 2].reshape(bsz, NSA_H, t_len, nsb, 2).sum(-1)
    cur = qpos // SEL_BLOCK
    imp = jnp.where(jnp.arange(nsb)[None, :] < cur[:, None], imp, -1.0)
    _, idx = lax.top_k(imp, kp)
    hit = (idx[..., None] == jnp.arange(nsb)) & (idx < cur[:, None])[..., None]
    member = jnp.any(hit, axis=-2).astype(F32).reshape(bsz, NSA_H * t_len, nsb)
    return _sample_decode(zs, o_c.reshape(bsz * t_len, MIX_W), member, c, page_table, dec, bsz, t_len, past_len)


def _block_ones(gs):
    i = np.arange(256)
    return jnp.asarray((i[:, None] // gs) == (i[None, :] // gs), BF16)


def _tile_heads(v, reps):
    return jnp.tile(v.astype(F32), reps).reshape(1, -1)


def _prep_layer(lw, l):
    m64, m32 = _block_ones(64), _block_ones(32)
    w_in = lw['w_in']
    src = np.cumsum([0, 256, 256, 256, 256, 256, 256, 256, 12, RW_P, 256, 256, 256, 256, 256, 256, 4096])
    names = ['q', 'kc', 'vc', 'ks', 'vs', 'kw', 'vw', 'ng', 'rw', 'bg', 'cg', 'xc', 'qd', 'kd', 'vd', 'gate']
    dst = dict(q=OFF_Q, kc=OFF_KC, vc=OFF_VC, ks=OFF_KS, vs=OFF_VS, kw=OFF_KW, vw=OFF_VW, ng=OFF_NG, rw=OFF_RW,
               bg=OFF_BG, cg=OFF_CG, xc=OFF_XC, qd=OFF_QD, kd=OFF_KD, vd=OFF_VD, gate=OFF_GATE)
    order = sorted(names, key=lambda nm: dst[nm])
    pieces = []
    pos = 0
    for nm in order:
        k = names.index(nm)
        assert dst[nm] == pos, (nm, dst[nm], pos)
        piece = w_in[:, int(src[k]):int(src[k + 1])]
        if nm == 'ng':
            piece = jnp.pad(piece, ((0, 0), (0, 128 - 3 * NSA_H)))
        pieces.append(piece)
        pos += piece.shape[1]
    assert pos == P_PAD
    w_pad = jnp.concatenate(pieces, axis=1).astype(BF16)
    gains = jnp.concatenate([
        _tile_heads(lw['nsa_qn'], 4) * (HD ** -0.5), _tile_heads(lw['nsa_kn'][1], 4), _tile_heads(lw['nsa_kn'][2], 4),
        _tile_heads(lw['df_qn'], 8) * (DF_DQK ** -0.5), _tile_heads(lw['df_kn'], 8),
        jnp.zeros((3, 256), F32)], axis=0)
    eye4 = jnp.eye(NSA_H, dtype=F32)
    bd = lambda w: jnp.einsum('hg,ide->ihdge', eye4, w).reshape(CMP_K, 256).astype(BF16)
    lam_init = 0.8 - 0.6 * math.exp(-0.3 * l)
    lam = (jnp.exp(jnp.sum(lw['df_lq1'] * lw['df_lk1'])) - jnp.exp(jnp.sum(lw['df_lq2'] * lw['df_lk2'])) + lam_init)
    lora = lambda w, a, b: jnp.zeros((128, 256), F32).at[a:b].set(w).astype(BF16)
    seqw = dict(mu=lw['rw_mu'].reshape(1, RW_P), w0=lw['rw_w0'].reshape(1, 256), a0=lw['rw_a0'].reshape(1, 256),
                lw=lora(lw['rw_w2'], 0, 32), la=lora(lw['rw_a2'], 32, 64), lg=lora(lw['rw_g2'], 64, 128),
                kk=lw['rw_kk'].reshape(1, 256), ka=lw['rw_ka'].reshape(1, 256),
                cw=jnp.pad(lw['conv_w'], ((0, 8 - CONV_K), (0, 0))), cb=lw['conv_b'].reshape(1, 256), m64=m64)
    mw = dict(wbr=lw['w_br'].astype(BF16), wo=lw['w_o'].astype(BF16), lnw=lw['rw_lnw'].reshape(1, 256),
              lnb=lw['rw_lnb'].reshape(1, 256), rk=lw['rw_rk'].reshape(1, 256), m64=m64,
              nx=lw['norm_x'].reshape(1, D_MODEL), wxq=lw['w_xq'].astype(BF16),
              qg=_tile_heads(lw['x_qn'], 4) * (HD ** -0.5))
    keys = lw['pk_keys'].reshape(PK_H * 2, N_KEYS, PK_DK // 2)
    eye16 = jnp.eye(PK_H * 2, dtype=F32)
    kbd = jnp.einsum('gf,gkd->gkfd', eye16, keys).reshape(PK_H * 2 * N_KEYS, PK_H * PK_DK).astype(BF16)
    pw = dict(wq=lw['pk_wq'].astype(BF16), kbd=kbd, u=_pack_table(lw['pk_u']), v=_pack_table(lw['pk_v']))
    return dict(
        g_mix=lw['norm_mix'].reshape(1, D_MODEL), w_pad=w_pad, gains=gains, m64=m64, m32=m32,
        wk_bd=bd(lw['nsa_ck_w']), wv_bd=bd(lw['nsa_cv_w']),
        pek=jnp.tile(lw['nsa_ck_pe'], (1, NSA_H)).reshape(1, CMP_K),
        pev=jnp.tile(lw['nsa_cv_pe'], (1, NSA_H)).reshape(1, CMP_K),
        gk0=_tile_heads(lw['nsa_kn'][0], 4), lam=lam.reshape(1).astype(F32), lam_init=lam_init,
        df_gain=_tile_heads(lw['df_subln'], 4) * (1.0 - lam_init), seqw=seqw, mw=mw, pw=pw,
        g_mem=lw['norm_mem'].reshape(1, D_MODEL),
        wkv=jnp.concatenate([lw['w_xk'], lw['w_xv']], axis=1).astype(BF16),
        kg=_tile_heads(lw['x_kn'], 4), wxo=lw['w_xo'].astype(BF16), nf=lw['norm_ffn'].reshape(1, D_MODEL))


def _pad_rows(x, mult):
    n = x.shape[0]
    p = (-n) % mult
    return x if p == 0 else jnp.pad(x, ((0, p),) + ((0, 0),) * (x.ndim - 1))


def _tail(x2d, z, ys, rw, mk, mv, pl_, bsz, t_len):
    y_nsa, y_scan, y_cv, y_df = ys
    r, k, v, g = rw
    x1, qx = _merge(x2d, z, y_nsa, y_scan, r, k, v, g, y_cv, y_df, pl_['mw'])
    x2, hf = _xattn(qx, mk, mv, x1, pl_['wxo'], pl_['nf'], bsz, t_len)
    n = x2.shape[0]
    out = _peer(_pad_rows(hf, PEER_TILE), _pad_rows(x2, PEER_TILE), pl_['pw'])
    return out[:n]


def _prompt_layer(x2d, mem2d, pl_, bsz, t_len):
    n = bsz * t_len
    z, zb = _proj(x2d, pl_['g_mix'], pl_['w_pad'], pl_['gains'], pl_['m64'], pl_['m32'])
    kc_raw, vc_raw = z[:, OFF_KC:OFF_KC + 256], z[:, OFF_VC:OFF_VC + 256]
    cmp4 = _cmp(kc_raw.reshape(n // SEL_BLOCK, 2 * CMP_K), vc_raw.reshape(n // SEL_BLOCK, 2 * CMP_K),
                pl_['wk_bd'], pl_['wv_bd'], pl_['pek'], pl_['pev'], pl_['m64'], pl_['gk0'])
    nsb = t_len // SEL_BLOCK
    expand = jnp.asarray(np.arange(nsb)[:, None] == (np.arange(t_len)[None, :] // SEL_BLOCK), BF16)
    y_nsa = _nsa_prompt(z, zb, cmp4, expand, bsz, t_len)
    y_df = _diff_prompt(pl_['lam'], z, zb, pl_['m64'], pl_['df_gain'], bsz, t_len)
    seq = _seq(z, jnp.zeros((bsz, 8, RW_P), F32), jnp.zeros((bsz, 8, 256), F32), pl_['seqw'], bsz, t_len)
    r, w, k, v, am, bb, g, y_cv, ulast = seq
    y_scan, s_fin = _rwkv_scan((r, w, k, v, am, bb), jnp.zeros((bsz, RW_H, RW_N, RW_N), F32), bsz, t_len)
    mk, mv = _memkv(mem2d, pl_['g_mem'], pl_['wkv'], pl_['m64'], pl_['kg'])
    mlen = mem2d.shape[0] // bsz
    mk3, mv3 = mk.reshape(bsz, mlen, 256), mv.reshape(bsz, mlen, 256)
    x_out = _tail(x2d, z, (y_nsa, y_scan, y_cv, y_df), (r, k, v, g), mk3.astype(BF16), mv3.astype(BF16),
                  pl_, bsz, t_len)
    st = lambda off: z[:, off:off + 256].reshape(bsz, t_len, NSA_H, HD)
    wb = min(WINDOW, t_len)
    shift = z[:, OFF_RW:OFF_RW + RW_P].reshape(bsz, t_len, RW_P)[:, -1]
    states = (st(OFF_KC), st(OFF_VC), st(OFF_KS), st(OFF_VS), st(OFF_KW)[:, t_len - wb:], st(OFF_VW)[:, t_len - wb:],
              st(OFF_KD), st(OFF_VD), mk3.reshape(bsz, mlen, MX_H, HD), mv3.reshape(bsz, mlen, MX_H, HD),
              s_fin, shift, ulast[:, 8 - (CONV_K - 1):])
    return x_out, states


def _sample_layer(x2d, c, page_table, lw, pl_, bsz, t_len, past_len):
    z, _ = _proj(x2d, pl_['g_mix'], pl_['w_pad'], pl_['gains'], pl_['m64'], pl_['m32'])
    dec = dict(c['full'], lam=pl_['lam'], m64=pl_['m64'], df_gain=pl_['df_gain'])
    y_nsa, y_df = _sample_attn(z, c, page_table, lw, dec, bsz, t_len, past_len)
    shift0 = jnp.zeros((bsz, 8, RW_P), F32).at[:, 7].set(c['rwkv_shift'])
    conv0 = jnp.zeros((bsz, 8, 256), F32).at[:, 8 - (CONV_K - 1):].set(c['conv'])
    seq = _seq(z, shift0, conv0, pl_['seqw'], bsz, t_len)
    r, w, k, v, am, bb, g, y_cv, ulast = seq
    y_scan, s_fin = _rwkv_scan((r, w, k, v, am, bb), c['rwkv'], bsz, t_len)
    mk = c['mem_k'].reshape(bsz, -1, 256).astype(BF16)
    mv = c['mem_v'].reshape(bsz, -1, 256).astype(BF16)
    x_out = _tail(x2d, z, (y_nsa, y_scan, y_cv, y_df), (r, k, v, g), mk, mv, pl_, bsz, t_len)
    st = lambda off: z[:, off:off + 256].reshape(bsz, t_len, NSA_H, HD)
    shift = z[:, OFF_RW:OFF_RW + RW_P].reshape(bsz, t_len, RW_P)[:, -1]
    states = (st(OFF_KC), st(OFF_VC), st(OFF_KS), st(OFF_VS), st(OFF_KW), st(OFF_VW), st(OFF_KD), st(OFF_VD),
              s_fin, shift, ulast[:, 8 - (CONV_K - 1):])
    return x_out, states


def kernel(x_prompt, x_sample, mem_prompt, cache_cmp_k, cache_cmp_v, cache_sel_k, cache_sel_v, cache_win_k, cache_win_v, cache_diff_k, cache_diff_v, cache_mem_k, cache_mem_v, state_rwkv, state_rwkv_shift, state_conv, page_table, norm_mix, w_in, nsa_qn, nsa_kn, nsa_ck_w, nsa_ck_pe, nsa_cv_w, nsa_cv_pe, rw_mu, rw_w0, rw_w2, rw_a0, rw_a2, rw_g2, rw_kk, rw_ka, rw_rk, rw_lnw, rw_lnb, conv_w, conv_b, df_qn, df_kn, df_lq1, df_lk1, df_lq2, df_lk2, df_subln, w_br, w_o, norm_x, norm_mem, w_xq, w_xk, w_xv, x_qn, x_kn, w_xo, norm_ffn, pk_wq, pk_keys, pk_u, pk_v):
    bp, tp, _ = x_prompt.shape
    bs, ts, _ = x_sample.shape
    depth = w_in.shape[0]
    past_len = page_table.shape[1] * PAGE_SIZE
    weights = dict(norm_mix=norm_mix, w_in=w_in, nsa_qn=nsa_qn, nsa_kn=nsa_kn, nsa_ck_w=nsa_ck_w,
                   nsa_ck_pe=nsa_ck_pe, nsa_cv_w=nsa_cv_w, nsa_cv_pe=nsa_cv_pe, rw_mu=rw_mu, rw_w0=rw_w0,
                   rw_w2=rw_w2, rw_a0=rw_a0, rw_a2=rw_a2, rw_g2=rw_g2, rw_kk=rw_kk, rw_ka=rw_ka, rw_rk=rw_rk,
                   rw_lnw=rw_lnw, rw_lnb=rw_lnb, conv_w=conv_w, conv_b=conv_b, df_qn=df_qn, df_kn=df_kn,
                   df_lq1=df_lq1, df_lk1=df_lk1, df_lq2=df_lq2, df_lk2=df_lk2, df_subln=df_subln, w_br=w_br,
                   w_o=w_o, norm_x=norm_x, norm_mem=norm_mem, w_xq=w_xq, w_xk=w_xk, w_xv=w_xv, x_qn=x_qn,
                   x_kn=x_kn, w_xo=w_xo, norm_ffn=norm_ffn, pk_wq=pk_wq, pk_keys=pk_keys, pk_u=pk_u, pk_v=pk_v)
    xp = x_prompt.reshape(bp * tp, D_MODEL)
    xs = x_sample.reshape(bs * ts, D_MODEL)
    mem2d = mem_prompt.reshape(-1, D_MODEL)
    p_acc, s_acc = [], []
    for l in range(depth):
        lw = {k_: v_[l] for k_, v_ in weights.items()}
        pl_ = _prep_layer(lw, l)
        full = dict(l=l, sel_k=cache_sel_k, sel_v=cache_sel_v, diff_k=cache_diff_k, diff_v=cache_diff_v,
                    win_k=cache_win_k, win_v=cache_win_v)
        c = dict(cmp_k=cache_cmp_k[l], cmp_v=cache_cmp_v[l], mem_k=cache_mem_k[l], mem_v=cache_mem_v[l],
                 rwkv=state_rwkv[l], rwkv_shift=state_rwkv_shift[l], conv=state_conv[l], full=full)
        xp, ps = _prompt_layer(xp, mem2d, pl_, bp, tp)
        xs, ss = _sample_layer(xs, c, page_table, lw, pl_, bs, ts, past_len)
        p_acc.append(ps)
        s_acc.append(ss)
    p_out = [jnp.stack(a) for a in zip(*p_acc)]
    s_out = [jnp.stack(a) for a in zip(*s_acc)]
    return (xp.reshape(bp, tp, D_MODEL), xs.reshape(bs, ts, D_MODEL), *p_out, *s_out)
```

```python
import functools
import math

import jax
import jax.numpy as jnp
import numpy as np
from jax import lax
from jax.experimental import pallas as pl
from jax.experimental.pallas import tpu as pltpu

F32 = jnp.float32
BF16 = jnp.bfloat16
I32 = jnp.int32

D_MODEL = 1024
PAGE_SIZE = 128
N_BRANCH = 4
MIX_W = 256
NSA_H = 4
HD = 64
CMP_BLOCK = 32
SEL_BLOCK = 64
SEL_TOPK = 16
WINDOW = 512
RW_H = 4
RW_N = 64
DECAY_LORA = 32
AAA_LORA = 32
GATE_LORA = 64
RW_P = 896
CONV_K = 3
DF_H = 4
DF_DQK = 32
DF_DV = 64
MX_H = 4
PK_H = 8
N_KEYS = 128
PK_DK = 128
PK_TOPK = 16
EPS = 1e-6
GN_EPS = 64e-5
N_ALIBI = NSA_H + DF_H
ALIBI_SLOPES = tuple(2.0 ** (-8.0 * (i + 1) / N_ALIBI) for i in range(N_ALIBI))
NSA_SLOPES = ALIBI_SLOPES[0::2]
DF_SLOPES = ALIBI_SLOPES[1::2]

LANES = 128
VMEM_LIMIT = 56 * 1024 * 1024

OFF_GATE = 0
OFF_VD = 4096
OFF_NG = 4352
OFF_RW = 4480
OFF_Q = 5376
OFF_KS = 5632
OFF_KW = 5888
OFF_QD = 6144
OFF_KD = 6400
OFF_KC = 6656
OFF_VC = 6912
OFF_VS = 7168
OFF_VW = 7424
OFF_BG = 7680
OFF_CG = 7936
OFF_XC = 8192
P_PAD = 8448
ZB_KS, ZB_KW, ZB_KD, ZB_VS, ZB_VW, ZB_VD = range(6)


def _dot(a, b):
    return jnp.dot(a, b, preferred_element_type=F32)


def _dot_nt(a, b):
    return lax.dot_general(a, b, (((1,), (1,)), ((), ())), preferred_element_type=F32)


def _seg_sum(x, m):
    hi = x.astype(BF16)
    lo = (x - hi.astype(F32)).astype(BF16)
    return _dot(hi, m) + _dot(lo, m)


def _iota(shape, dim):
    return lax.broadcasted_iota(I32, shape, dim)


def _row_tile(n, pref):
    t = pref
    while t > 8 and n % t:
        t //= 2
    assert n % t == 0, (n, pref)
    return t


def _cparams(sem):
    return pltpu.CompilerParams(dimension_semantics=sem, vmem_limit_bytes=VMEM_LIMIT)


def _vmem_full():
    return pl.BlockSpec(memory_space=pltpu.VMEM)


_NORMED = ((OFF_Q, 64), (OFF_KS, 64), (OFF_KW, 64), (OFF_QD, 32), (OFF_KD, 32))
_ZB_SRC = ((OFF_KS, ZB_KS), (OFF_KW, ZB_KW), (OFF_KD, ZB_KD), (OFF_VS, ZB_VS), (OFF_VW, ZB_VW), (OFF_VD, ZB_VD))


def _proj_plan():
    zb_of = dict(_ZB_SRC)
    normed = dict(_NORMED)
    special = sorted(set(zb_of) | set(normed))
    plan, c0 = [], 0
    for s in special + [P_PAD]:
        while c0 < s:
            w = min(512, s - c0)
            plan.append((c0, w, 0, -1))
            c0 += w
        if s < P_PAD:
            plan.append((s, 256, normed.get(s, 0), zb_of.get(s, -1)))
            c0 = s + 256
    return plan


def _proj_kernel(x_ref, g_ref, w_ref, gains_ref, m64_ref, m32_ref, z_ref, zb_ref):
    x = x_ref[...]
    h = x * lax.rsqrt(jnp.mean(x * x, axis=-1, keepdims=True) + EPS) * g_ref[...]
    hb = h.astype(BF16)
    norm_row = {o: r for r, (o, _) in enumerate(_NORMED)}
    for c0, w, gs, slot in _proj_plan():
        z = _dot(hb, w_ref[:, c0:c0 + w])
        if gs:
            m = m64_ref[...] if gs == 64 else m32_ref[...]
            ss = _seg_sum(z * z, m)
            r = norm_row[c0]
            z = z * lax.rsqrt(ss * (1.0 / gs) + EPS) * gains_ref[r:r + 1, :]
        z_ref[:, c0:c0 + w] = z
        if slot >= 0:
            zb_ref[:, slot * 256:(slot + 1) * 256] = z.astype(BF16)


def _proj(x2d, g, w_pad, gains, m64, m32):
    n = x2d.shape[0]
    tm = _row_tile(n, 256)
    return pl.pallas_call(
        _proj_kernel,
        grid=(n // tm,),
        in_specs=[pl.BlockSpec((tm, D_MODEL), lambda i: (i, 0)),
                  pl.BlockSpec((1, D_MODEL), lambda i: (0, 0)),
                  _vmem_full(),
                  pl.BlockSpec((8, 256), lambda i: (0, 0)),
                  pl.BlockSpec((256, 256), lambda i: (0, 0)),
                  pl.BlockSpec((256, 256), lambda i: (0, 0))],
        out_specs=[pl.BlockSpec((tm, P_PAD), lambda i: (i, 0)),
                   pl.BlockSpec((tm, 1536), lambda i: (i, 0))],
        out_shape=[jax.ShapeDtypeStruct((n, P_PAD), F32), jax.ShapeDtypeStruct((n, 1536), BF16)],
        compiler_params=_cparams(("parallel",)),
    )(x2d, g, w_pad, gains, m64, m32)


CMP_K = CMP_BLOCK * 256


def _cmp_kernel(kc_ref, vc_ref, wk_ref, wv_ref, pek_ref, pev_ref, m64_ref, gk_ref,
                kce_ref, kco_ref, vce_ref, vco_ref):
    for src, w_ref, pe_ref, outs, is_k in ((kc_ref, wk_ref, pek_ref, (kce_ref, kco_ref), True),
                                           (vc_ref, wv_ref, pev_ref, (vce_ref, vco_ref), False)):
        for par in range(2):
            rows = src[:, par * CMP_K:(par + 1) * CMP_K] + pe_ref[...]
            acc = _dot(rows.astype(BF16), w_ref[...])
            if is_k:
                ss = _seg_sum(acc * acc, m64_ref[...])
                acc = acc * lax.rsqrt(ss * (1.0 / HD) + EPS) * gk_ref[...]
            outs[par][...] = acc.astype(BF16)


def _cmp(kc2, vc2, wk_bd, wv_bd, pek, pev, m64, gk):
    nr = kc2.shape[0]
    rb = _row_tile(nr, 64)
    out = jax.ShapeDtypeStruct((nr, 256), BF16)
    ospec = pl.BlockSpec((rb, 256), lambda i: (i, 0))
    ispec = pl.BlockSpec((rb, 2 * CMP_K), lambda i: (i, 0))
    par = lambda s: pl.BlockSpec(s, lambda i: (0,) * len(s))
    return pl.pallas_call(
        _cmp_kernel,
        grid=(nr // rb,),
        in_specs=[ispec, ispec, par((CMP_K, 256)), par((CMP_K, 256)), par((1, CMP_K)), par((1, CMP_K)),
                  par((256, 256)), par((1, 256))],
        out_specs=[ospec, ospec, ospec, ospec],
        out_shape=[out, out, out, out],
        compiler_params=_cparams(("parallel",)),
    )(kc2, vc2, wk_bd, wv_bd, pek, pev, m64, gk)


def _softmax_rows(s):
    m = jnp.max(s, axis=-1, keepdims=True)
    e = jnp.exp(s - m)
    return e / jnp.sum(e, axis=-1, keepdims=True)


def _nsa_kernel(q_ref, ng_ref, kce_ref, kco_ref, vce_ref, vco_ref, ks_ref, vs_ref, kw_ref, vw_ref, ex_ref,
                o_ref, *, t_len, tq, kp, band):
    i = pl.program_id(1)
    nsb = t_len // SEL_BLOCK
    q = q_ref[...]
    sig = jax.nn.sigmoid(ng_ref[...])
    lane = _iota((1, 256), 1)
    qpos = i * tq + _iota((tq, 1), 0)
    cur = jnp.right_shift(qpos, 6)
    jb = _iota((1, nsb), 1)
    kpos = _iota((1, t_len), 1)
    dist = (qpos - kpos).astype(F32)
    curf = jnp.where(jnp.right_shift(kpos, 6) == cur, jnp.where(kpos <= qpos, 1.0, 0.0), 0.0)
    de = (qpos - (jb * SEL_BLOCK + (CMP_BLOCK - 1))).astype(F32)
    do = (qpos - (jb * SEL_BLOCK + (SEL_BLOCK - 1))).astype(F32)
    past = jb < cur
    start = jnp.clip(i * tq - WINDOW, 0, t_len - band)
    start = pl.multiple_of(start, tq)
    kposw = start + _iota((1, band), 1)
    dw = qpos - kposw
    okw = jnp.where(dw >= 0, jnp.where(dw < WINDOW, 1.0, 0.0), 0.0) > 0.5
    dwf = dw.astype(F32)
    kwb = kw_ref[pl.ds(start, band), :]
    vwb = vw_ref[pl.ds(start, band), :]
    acc = jnp.zeros((tq, 256), F32)
    for h in range(NSA_H):
        hm = jnp.right_shift(lane, 6) == h
        qh = jnp.where(hm, q, 0.0).astype(BF16)
        sl = NSA_SLOPES[h]
        se = jnp.where(de >= 0, _dot_nt(qh, kce_ref[...]) - sl * de, -jnp.inf)
        so = jnp.where(do >= 0, _dot_nt(qh, kco_ref[...]) - sl * do, -jnp.inf)
        m = jnp.maximum(jnp.max(se, axis=-1, keepdims=True), jnp.max(so, axis=-1, keepdims=True))
        m = jnp.where(m > -jnp.inf, m, 0.0)
        pe = jnp.where(de >= 0, jnp.exp(se - m), 0.0)
        po = jnp.where(do >= 0, jnp.exp(so - m), 0.0)
        den = jnp.sum(pe, axis=-1, keepdims=True) + jnp.sum(po, axis=-1, keepdims=True)
        den = jnp.where(den > 0, den, 1.0)
        pe = pe / den
        po = po / den
        o_c = _dot(pe.astype(BF16), vce_ref[...]) + _dot(po.astype(BF16), vco_ref[...])
        imp = jnp.where(past, pe + po, -1.0)
        rank = jnp.zeros((tq, nsb), F32)
        for j2 in range(nsb):
            col = imp[:, j2:j2 + 1]
            before = jnp.where(jb > j2, 1.0, 0.0)
            rank = rank + jnp.where(col > imp, 1.0, jnp.where(col == imp, before, 0.0))
        member = jnp.where(past, jnp.where(rank < kp, 1.0, 0.0), 0.0)
        allowed = (_dot(member.astype(BF16), ex_ref[...]) + curf) > 0.5
        s = jnp.where(allowed, _dot_nt(qh, ks_ref[...]) - sl * dist, -jnp.inf)
        o_s = _dot(_softmax_rows(s).astype(BF16), vs_ref[...])
        s = jnp.where(okw, _dot_nt(qh, kwb) - sl * dwf, -jnp.inf)
        o_w = _dot(_softmax_rows(s).astype(BF16), vwb)
        y = (sig[:, h:h + 1] * o_c + sig[:, NSA_H + h:NSA_H + h + 1] * o_s
             + sig[:, 2 * NSA_H + h:2 * NSA_H + h + 1] * o_w)
        acc = acc + jnp.where(hm, y, 0.0)
    o_ref[...] = acc


def _nsa_prompt(z, zb, cmp4, expand, bsz, t_len):
    tq = 128
    nt = t_len // tq
    nsb = t_len // SEL_BLOCK
    kp = min(SEL_TOPK - 1, nsb)
    band = min(WINDOW + tq, t_len)
    kv = lambda c: pl.BlockSpec((t_len, 256), lambda b, i: (b, c))
    cs = pl.BlockSpec((nsb, 256), lambda b, i: (b, 0))
    return pl.pallas_call(
        functools.partial(_nsa_kernel, t_len=t_len, tq=tq, kp=kp, band=band),
        grid=(bsz, nt),
        in_specs=[pl.BlockSpec((tq, 256), lambda b, i: (b * nt + i, OFF_Q // 256)),
                  pl.BlockSpec((tq, 128), lambda b, i: (b * nt + i, OFF_NG // 128)),
                  cs, cs, cs, cs, kv(ZB_KS), kv(ZB_VS), kv(ZB_KW), kv(ZB_VW),
                  pl.BlockSpec((nsb, t_len), lambda b, i: (0, 0))],
        out_specs=pl.BlockSpec((tq, 256), lambda b, i: (b * nt + i, 0)),
        out_shape=jax.ShapeDtypeStruct((bsz * t_len, 256), F32),
        compiler_params=_cparams(("parallel", "parallel")),
    )(z, z, *cmp4, zb, zb, zb, zb, expand)


def _diff_kernel(lam_ref, q_ref, k_ref, v_ref, m64_ref, gain_ref, o_ref, *, t_len, tq):
    i = pl.program_id(1)
    lam = lam_ref[0]
    lane = _iota((1, 256), 1)
    qpos = i * tq + _iota((tq, 1), 0)

    def attend(klen):
        q = q_ref[...]
        kpos = _iota((1, klen), 1)
        causal = kpos <= qpos
        dist = (qpos - kpos).astype(F32)
        acc = jnp.zeros((tq, 256), F32)
        for h in range(DF_H):
            ps = []
            for c in range(2):
                lm = jnp.right_shift(lane, 5) == (2 * h + c)
                qh = jnp.where(lm, q, 0.0).astype(BF16)
                s = jnp.where(causal, _dot_nt(qh, k_ref[0:klen, :]) - DF_SLOPES[h] * dist, -jnp.inf)
                ps.append(_softmax_rows(s))
            pd = (ps[0] - lam * ps[1]).astype(BF16)
            acc = acc + jnp.where(jnp.right_shift(lane, 6) == h, _dot(pd, v_ref[0:klen, :]), 0.0)
        ss = _seg_sum(acc * acc, m64_ref[...])
        o_ref[...] = acc * lax.rsqrt(ss * (1.0 / DF_DV) + EPS) * gain_ref[...]

    n_span = 4 if t_len % (4 * tq) == 0 else 1
    span = t_len // n_span
    last = (i + 1) * tq
    for b in range(n_span):
        pl.when(jnp.logical_and(last > b * span, last <= (b + 1) * span))(functools.partial(attend, (b + 1) * span))


def _diff_prompt(lam, z, zb, m64, gain, bsz, t_len):
    tq = 128
    nt = t_len // tq
    kv = lambda c: pl.BlockSpec((t_len, 256), lambda b, i: (b, c))
    return pl.pallas_call(
        functools.partial(_diff_kernel, t_len=t_len, tq=tq),
        grid=(bsz, nt),
        in_specs=[pl.BlockSpec(memory_space=pltpu.SMEM),
                  pl.BlockSpec((tq, 256), lambda b, i: (b * nt + i, OFF_QD // 256)),
                  kv(ZB_KD), kv(ZB_VD),
                  pl.BlockSpec((256, 256), lambda b, i: (0, 0)),
                  pl.BlockSpec((1, 256), lambda b, i: (0, 0))],
        out_specs=pl.BlockSpec((tq, 256), lambda b, i: (b * nt + i, 0)),
        out_shape=jax.ShapeDtypeStruct((bsz * t_len, 256), F32),
        compiler_params=_cparams(("parallel", "parallel")),
    )(lam, z, zb, zb, m64, gain)


def _shift_rows(x, first_rows, k):
    n = x.shape[0]
    rolled = pltpu.roll(x, k, 0)
    row = _iota((n, 1), 0)
    out = rolled
    for r in range(k):
        out = jnp.where(row == r, first_rows[8 - k + r:8 - k + r + 1, :], out)
    return out


def _seq_kernel(rw_ref, rwp_ref, bg_ref, cg_ref, xc_ref, cgp_ref, xcp_ref, sh0_ref, cv0_ref,
                mu_ref, w0_ref, a0_ref, lw_ref, la_ref, lg_ref, kk_ref, ka_ref, cw_ref, cb_ref, m64_ref,
                r_ref, w_ref, k_ref, v_ref, am_ref, bb_ref, g_ref, ycv_ref, ulast_ref, *, ts):
    i = pl.program_id(1)
    x = rw_ref[...]
    first = jnp.where(i == 0, sh0_ref[0], rwp_ref[...])
    prev = _shift_rows(x, first, 1)
    xm = x + (prev - x) * mu_ref[...]
    r = xm[:, 0:256]
    k = xm[:, 256:512]
    v = xm[:, 512:768]
    lo = xm[:, 768:896]
    wlin = w0_ref[...] + _dot(jnp.tanh(lo).astype(BF16), lw_ref[...])
    w = -jax.nn.softplus(-wlin) - 0.5
    decay = jnp.exp(-jnp.exp(w))
    a = jax.nn.sigmoid(a0_ref[...] + _dot(lo.astype(BF16), la_ref[...]))
    g = _dot(jax.nn.sigmoid(lo).astype(BF16), lg_ref[...])
    kk = k * kk_ref[...]
    nrm = jnp.sqrt(_seg_sum(kk * kk, m64_ref[...]))
    kk = kk / jnp.maximum(nrm, 1e-12)
    r_ref[...] = r
    w_ref[...] = decay
    k_ref[...] = k * (1.0 + (a - 1.0) * ka_ref[...])
    v_ref[...] = v
    am_ref[...] = -kk
    bb_ref[...] = kk * a
    g_ref[...] = g
    u = cg_ref[...] * xc_ref[...]
    ufirst = jnp.where(i == 0, cv0_ref[0], cgp_ref[...] * xcp_ref[...])
    u1 = _shift_rows(u, ufirst, 1)
    u2 = _shift_rows(u, ufirst, 2)
    y = cb_ref[...] + u2 * cw_ref[0:1, :] + u1 * cw_ref[1:2, :] + u * cw_ref[2:3, :]
    ycv_ref[...] = bg_ref[...] * y
    ulast_ref[0] = u[ts - 8:ts, :]


def _seq(z, shift0, conv0, lwp, bsz, t_len):
    ts = _row_tile(t_len, 512)
    nt = t_len // ts
    n = bsz * t_len
    row = lambda w, c: pl.BlockSpec((ts, w), lambda b, i: (b * nt + i, c))
    prv = lambda w, c: pl.BlockSpec((8, w), lambda b, i: (jnp.maximum((b * nt + i) * (ts // 8) - 1, 0), c))
    par = lambda s: pl.BlockSpec(s, lambda b, i: (0,) * len(s))
    o256 = pl.BlockSpec((ts, 256), lambda b, i: (b * nt + i, 0))
    sd = jax.ShapeDtypeStruct((n, 256), F32)
    return pl.pallas_call(
        functools.partial(_seq_kernel, ts=ts),
        grid=(bsz, nt),
        in_specs=[row(RW_P, OFF_RW // RW_P), prv(RW_P, OFF_RW // RW_P),
                  row(256, OFF_BG // 256), row(256, OFF_CG // 256), row(256, OFF_XC // 256),
                  prv(256, OFF_CG // 256), prv(256, OFF_XC // 256),
                  pl.BlockSpec((1, 8, RW_P), lambda b, i: (b, 0, 0)),
                  pl.BlockSpec((1, 8, 256), lambda b, i: (b, 0, 0)),
                  par((1, RW_P)), par((1, 256)), par((1, 256)), par((128, 256)), par((128, 256)), par((128, 256)),
                  par((1, 256)), par((1, 256)), par((8, 256)), par((1, 256)), par((256, 256))],
        out_specs=[o256] * 8 + [pl.BlockSpec((1, 8, 256), lambda b, i: (b, 0, 0))],
        out_shape=[sd] * 8 + [jax.ShapeDtypeStruct((bsz, 8, 256), F32)],
        compiler_params=_cparams(("parallel", "arbitrary")),
    )(z, z, z, z, z, z, z, shift0, conv0, lwp['mu'], lwp['w0'], lwp['a0'], lwp['lw'], lwp['la'], lwp['lg'],
      lwp['kk'], lwp['ka'], lwp['cw'], lwp['cb'], lwp['m64'])


def _scan_kernel(s0_ref, am_ref, w_ref, bb_ref, k_ref, r_ref, v_ref, y_ref, s_ref, *, tc, ng, ih):
    @pl.when(pl.program_id(0) == 0)
    def _():
        s_ref[...] = s0_ref[...]

    def step(t, carry):
        def group(g, c2):
            h = g // ih
            hs = pl.multiple_of(h * RW_N, RW_N)
            st = s_ref[g]
            a_c = am_ref[t, pl.ds(hs, RW_N), :]
            sa = jnp.sum(st * a_c, axis=0, keepdims=True)
            vrow = v_ref[t, pl.ds(g, 1), :]
            st = (st * w_ref[t, pl.ds(hs, RW_N), :] + sa * bb_ref[t, pl.ds(hs, RW_N), :]
                  + vrow * k_ref[t, pl.ds(hs, RW_N), :])
            s_ref[g] = st
            y_ref[t, pl.ds(g, 1), :] = jnp.sum(st * r_ref[t, pl.ds(hs, RW_N), :], axis=0, keepdims=True)
            return c2

        return lax.fori_loop(0, ng, group, carry, unroll=4)

    lax.fori_loop(0, tc, step, 0)


def _scan(s0, am, w, bb, k, r, v, t_len, ih):
    ng = RW_H * ih
    tc = _row_tile(t_len, 16)
    col = pl.BlockSpec((tc, 256, LANES), lambda i: (i, 0, 0))
    rowv = pl.BlockSpec((tc, ng, LANES), lambda i: (i, 0, 0))
    st = pl.BlockSpec((ng, RW_N, LANES), lambda i: (0, 0, 0))
    return pl.pallas_call(
        functools.partial(_scan_kernel, tc=tc, ng=ng, ih=ih),
        grid=(t_len // tc,),
        in_specs=[st, col, col, col, col, col, rowv],
        out_specs=[rowv, st],
        out_shape=[jax.ShapeDtypeStruct((t_len, ng, LANES), F32), jax.ShapeDtypeStruct((ng, RW_N, LANES), F32)],
        compiler_params=_cparams(("arbitrary",)),
    )(s0, am, w, bb, k, r, v)


def _to_cols(x, bsz, t_len, il):
    x = x.reshape(bsz, t_len, 256).transpose(1, 2, 0)
    return jnp.tile(x, (1, 1, il))


def _to_rows(x, bsz, t_len, il, ih):
    x = x.reshape(bsz, t_len, RW_H, ih, il).transpose(1, 2, 3, 4, 0)
    return x.reshape(t_len, RW_H * ih, il * bsz)


def _from_rows(y, bsz, t_len, il, ih):
    y = y.reshape(t_len, RW_H, ih, il, bsz).transpose(4, 0, 1, 2, 3)
    return y.reshape(bsz * t_len, 256)


def _state_in(s, bsz, il, ih):
    s = s.reshape(bsz, RW_H, ih, il, RW_N).transpose(1, 2, 4, 3, 0)
    return s.reshape(RW_H * ih, RW_N, il * bsz)


def _state_out(s, bsz, il, ih):
    s = s.reshape(RW_H, ih, RW_N, il, bsz).transpose(4, 0, 1, 3, 2)
    return s.reshape(bsz, RW_H, RW_N, RW_N)


def _rwkv_scan(seq_out, s0, bsz, t_len):
    r, w, k, v, am, bb = seq_out
    il = LANES // bsz
    ih = RW_N // il
    cols = [_to_cols(a, bsz, t_len, il) for a in (am, w, bb, k, r)]
    y, s_fin = _scan(_state_in(s0, bsz, il, ih), *cols, _to_rows(v, bsz, t_len, il, ih), t_len, ih)
    return _from_rows(y, bsz, t_len, il, ih), _state_out(s_fin, bsz, il, ih)


def _merge_kernel(x_ref, gate_ref, ynsa_ref, yrw_ref, r_ref, k_ref, v_ref, g_ref, ycv_ref, ydf_ref,
                  wbr_ref, wo_ref, lnw_ref, lnb_ref, rk_ref, m64_ref, nx_ref, wxq_ref, qg_ref,
                  x1_ref, qx_ref):
    m64 = m64_ref[...]
    y = yrw_ref[...]
    mu = _seg_sum(y, m64) * (1.0 / RW_N)
    d = y - mu
    var = _seg_sum(d * d, m64) * (1.0 / RW_N)
    yn = d * lax.rsqrt(var + GN_EPS) * lnw_ref[...] + lnb_ref[...]
    v = v_ref[...]
    bonus = _seg_sum(r_ref[...] * k_ref[...] * rk_ref[...], m64) * v
    y_rw = (yn + bonus) * g_ref[...]
    ys = (ynsa_ref[...], y_rw, ycv_ref[...], ydf_ref[...])
    zsum = None
    for b in range(N_BRANCH):
        t = jax.nn.sigmoid(gate_ref[:, b * D_MODEL:(b + 1) * D_MODEL]) * _dot(ys[b].astype(BF16), wbr_ref[b])
        zsum = t if zsum is None else zsum + t
    x1 = x_ref[...] + _dot(zsum.astype(BF16), wo_ref[...])
    x1_ref[...] = x1
    hx = x1 * lax.rsqrt(jnp.mean(x1 * x1, axis=-1, keepdims=True) + EPS) * nx_ref[...]
    qx = _dot(hx.astype(BF16), wxq_ref[...])
    ss = _seg_sum(qx * qx, m64)
    qx_ref[...] = qx * lax.rsqrt(ss * (1.0 / HD) + EPS) * qg_ref[...]


def _merge(x2d, z, ynsa, yrw, r, k, v, g, ycv, ydf, mw):
    n = x2d.shape[0]
    tm = _row_tile(n, 256)
    row = lambda w, c=0: pl.BlockSpec((tm, w), lambda i: (i, c))
    par = lambda s: pl.BlockSpec(s, lambda i: (0,) * len(s))
    return pl.pallas_call(
        _merge_kernel,
        grid=(n // tm,),
        in_specs=[row(D_MODEL), row(4 * D_MODEL, OFF_GATE)] + [row(256)] * 8
                 + [par((4, 256, D_MODEL)), par((D_MODEL, D_MODEL)), par((1, 256)), par((1, 256)), par((1, 256)),
                    par((256, 256)), par((1, D_MODEL)), par((D_MODEL, 256)), par((1, 256))],
        out_specs=[row(D_MODEL), row(256)],
        out_shape=[jax.ShapeDtypeStruct((n, D_MODEL), F32), jax.ShapeDtypeStruct((n, 256), F32)],
        compiler_params=_cparams(("parallel",)),
    )(x2d, z, ynsa, yrw, r, k, v, g, ycv, ydf, mw['wbr'], mw['wo'], mw['lnw'], mw['lnb'], mw['rk'], mw['m64'],
      mw['nx'], mw['wxq'], mw['qg'])


def _memkv_kernel(x_ref, g_ref, w_ref, m64_ref, kg_ref, k_ref, v_ref):
    x = x_ref[...]
    h = (x * lax.rsqrt(jnp.mean(x * x, axis=-1, keepdims=True) + EPS) * g_ref[...]).astype(BF16)
    k = _dot(h, w_ref[:, 0:256])
    ss = _seg_sum(k * k, m64_ref[...])
    k_ref[...] = k * lax.rsqrt(ss * (1.0 / HD) + EPS) * kg_ref[...]
    v_ref[...] = _dot(h, w_ref[:, 256:512])


def _memkv(mem2d, g, wkv, m64, kg):
    n = mem2d.shape[0]
    tm = _row_tile(n, 256)
    par = lambda s: pl.BlockSpec(s, lambda i: (0,) * len(s))
    row = lambda w: pl.BlockSpec((tm, w), lambda i: (i, 0))
    sd = jax.ShapeDtypeStruct((n, 256), F32)
    return pl.pallas_call(
        _memkv_kernel,
        grid=(n // tm,),
        in_specs=[row(D_MODEL), par((1, D_MODEL)), par((D_MODEL, 512)), par((256, 256)), par((1, 256))],
        out_specs=[row(256), row(256)],
        out_shape=[sd, sd],
        compiler_params=_cparams(("parallel",)),
    )(mem2d, g, wkv, m64, kg)


def _xattn_kernel(qx_ref, mk_ref, mv_ref, x1_ref, wxo_ref, nf_ref, x2_ref, hf_ref):
    q = qx_ref[...]
    lane = _iota((1, 256), 1)
    mk = mk_ref[0]
    mv = mv_ref[0]
    acc = jnp.zeros(q.shape, F32)
    for h in range(MX_H):
        hm = jnp.right_shift(lane, 6) == h
        s = _dot_nt(jnp.where(hm, q, 0.0).astype(BF16), mk)
        acc = acc + jnp.where(hm, _dot(_softmax_rows(s).astype(BF16), mv), 0.0)
    x2 = x1_ref[...] + _dot(acc.astype(BF16), wxo_ref[...])
    x2_ref[...] = x2
    hf_ref[...] = x2 * lax.rsqrt(jnp.mean(x2 * x2, axis=-1, keepdims=True) + EPS) * nf_ref[...]


def _xattn(qx, mk, mv, x1, wxo, nf, bsz, t_len):
    tq = _row_tile(t_len, 256)
    nt = t_len // tq
    mlen = mk.shape[1]
    row = lambda w: pl.BlockSpec((tq, w), lambda b, i: (b * nt + i, 0))
    mem = pl.BlockSpec((1, mlen, 256), lambda b, i: (b, 0, 0))
    sd = jax.ShapeDtypeStruct((bsz * t_len, D_MODEL), F32)
    return pl.pallas_call(
        _xattn_kernel,
        grid=(bsz, nt),
        in_specs=[row(256), mem, mem, row(D_MODEL),
                  pl.BlockSpec((256, D_MODEL), lambda b, i: (0, 0)),
                  pl.BlockSpec((1, D_MODEL), lambda b, i: (0, 0))],
        out_specs=[row(D_MODEL), row(D_MODEL)],
        out_shape=[sd, sd],
        compiler_params=_cparams(("parallel", "parallel")),
    )(qx, mk, mv, x1, wxo, nf)


PEER_TILE = 128


def _topk_cols(s, ids, n_take, payload=None):
    big = PK_TOPK * PK_TOPK + N_KEYS
    vals, picks = [], []
    for _ in range(n_take):
        m = jnp.max(s, axis=0, keepdims=True)
        ix = jnp.min(jnp.where(s == m, ids, big), axis=0, keepdims=True)
        hit = ids == ix
        vals.append(m)
        picks.append(ix if payload is None else jnp.max(jnp.where(hit, payload, -1), axis=0, keepdims=True))
        s = jnp.where(hit, -jnp.inf, s)
    return jnp.concatenate(vals, axis=0), jnp.concatenate(picks, axis=0)


def _peer_score_kernel(hf_ref, wq_ref, kbd_ref, e_ref, g_ref):
    q = _dot(hf_ref[...].astype(BF16), wq_ref[...])
    st = _dot_nt(kbd_ref[...], q.astype(BF16))
    kid = _iota((N_KEYS, PEER_TILE), 0)
    half = PK_TOPK // 2
    cids, cmask = [], []
    for a in range(half):
        nb = PK_TOPK // (a + 1)
        rows = -(-nb // 8) * 8
        b_io = _iota((rows, PEER_TILE), 0)
        cids.append(a * PK_TOPK + b_io)
        cmask.append(b_io < nb)
    cids.append((half + _iota((half, PEER_TILE), 0)) * PK_TOPK)
    cid = jnp.concatenate(cids, axis=0)
    for h in range(PK_H):
        sv0, si0 = _topk_cols(st[(2 * h) * N_KEYS:(2 * h + 1) * N_KEYS, :], kid, PK_TOPK)
        sv1, si1 = _topk_cols(st[(2 * h + 1) * N_KEYS:(2 * h + 2) * N_KEYS, :], kid, PK_TOPK)
        cparts, eparts = [], []
        for a in range(half):
            rows = cids[a].shape[0]
            cparts.append(jnp.where(cmask[a], sv0[a:a + 1, :] + sv1[0:rows, :], -jnp.inf))
            eparts.append(si0[a:a + 1, :] * N_KEYS + si1[0:rows, :])
        cparts.append(sv0[half:, :] + sv1[0:1, :])
        eparts.append(si0[half:, :] * N_KEYS + si1[0:1, :])
        cand = jnp.concatenate(cparts, axis=0)
        ecand = jnp.concatenate(eparts, axis=0)
        cv, ev = _topk_cols(cand, cid, PK_TOPK, payload=ecand)
        ex = jnp.exp(cv - cv[0:1, :])
        g_ref[h * PK_TOPK:(h + 1) * PK_TOPK, :] = ex / jnp.sum(ex, axis=0, keepdims=True)
        e_ref[h * PK_TOPK:(h + 1) * PK_TOPK, :] = ev


def _peer_score(hf, wq, kbd):
    n = hf.shape[0]
    nt = n // PEER_TILE
    nsel = PK_H * PK_TOPK
    out = pl.BlockSpec((nsel, PEER_TILE), lambda i: (i, 0))
    return pl.pallas_call(
        _peer_score_kernel,
        grid=(nt,),
        in_specs=[pl.BlockSpec((PEER_TILE, D_MODEL), lambda i: (i, 0)),
                  pl.BlockSpec((D_MODEL, PK_H * PK_DK), lambda i: (0, 0)),
                  pl.BlockSpec((PK_H * 2 * N_KEYS, PK_H * PK_DK), lambda i: (0, 0))],
        out_specs=[out, out],
        out_shape=[jax.ShapeDtypeStruct((nt * nsel, PEER_TILE), I32),
                   jax.ShapeDtypeStruct((nt * nsel, PEER_TILE), F32)],
        compiler_params=_cparams(("parallel",)),
    )(hf, wq, kbd)


def _unpack_pair(wd):
    lo = pltpu.bitcast(jnp.left_shift(wd, 16), F32)
    hi = pltpu.bitcast(jnp.bitwise_and(wd, jnp.uint32(0xFFFF0000)), F32)
    return lo, hi


PEER_GT = 64
NSEL = PK_H * PK_TOPK
HALF = D_MODEL // 2
QROWS = HALF // LANES


SUBL = 2 * QROWS


def _peer_u_kernel(row_ref, par_ref, x_ref, tab_ref, act_ref, a_ref):
    def token(n, carry):
        xrow = x_ref[pl.ds(n, 1), :]
        rows = [xrow[:, s * LANES:(s + 1) * LANES] for s in range(SUBL)]
        xl = jnp.concatenate(rows[:QROWS] + rows[:QROWS], axis=0)
        xh = jnp.concatenate(rows[QROWS:] + rows[QROWS:], axis=0)
        base = n * NSEL
        for j in range(NSEL):
            r = pl.multiple_of(row_ref[0, 0, base + j], SUBL)
            lo, hi = _unpack_pair(tab_ref[pl.ds(r, SUBL), :])
            a_ref[j * SUBL:(j + 1) * SUBL, :] = lo * xl + hi * xh
        halves = []
        for s0 in (0, QROWS):
            acc = a_ref[pl.ds(s0, NSEL, stride=SUBL), :]
            for s in range(s0 + 1, s0 + QROWS):
                acc = acc + a_ref[pl.ds(s, NSEL, stride=SUBL), :]
            halves.append(jnp.sum(acc.T, axis=0, keepdims=True))
        act_ref[pl.ds(n, 1), :] = jnp.where(par_ref[pl.ds(n, 1), :] == 0, halves[0], halves[1])
        return carry

    lax.fori_loop(0, PEER_GT, token, 0)


def _peer_u(row3, par, hf, tab):
    nt = row3.shape[0]
    return pl.pallas_call(
        _peer_u_kernel,
        grid=(nt,),
        in_specs=[pl.BlockSpec((1, 1, PEER_GT * NSEL), lambda i: (i, 0, 0), memory_space=pltpu.SMEM),
                  pl.BlockSpec((PEER_GT, NSEL), lambda i: (i, 0)),
                  pl.BlockSpec((PEER_GT, D_MODEL), lambda i: (i, 0)),
                  _vmem_full()],
        out_specs=pl.BlockSpec((PEER_GT, NSEL), lambda i: (i, 0)),
        out_shape=jax.ShapeDtypeStruct((nt * PEER_GT, NSEL), F32),
        scratch_shapes=[pltpu.VMEM((NSEL * SUBL, LANES), F32)],
        compiler_params=_cparams(("parallel",)),
    )(row3, par, hf, tab)


def _peer_v_kernel(row_ref, c8_ref, x_ref, tab_ref, o_ref):
    low = _iota((SUBL, LANES), 0) < QROWS
    n_acc = 4

    def token(n, carry):
        base = n * NSEL
        c8 = c8_ref[n]
        alo = [jnp.zeros((SUBL, LANES), F32) for _ in range(n_acc)]
        ahi = [jnp.zeros((SUBL, LANES), F32) for _ in range(n_acc)]
        for j in range(NSEL):
            r = pl.multiple_of(row_ref[0, 0, base + j], SUBL)
            lo, hi = _unpack_pair(tab_ref[pl.ds(r, SUBL), :])
            cm = jnp.broadcast_to(c8[:, j:j + 1], (SUBL, LANES))
            alo[j % n_acc] = alo[j % n_acc] + cm * lo
            ahi[j % n_acc] = ahi[j % n_acc] + cm * hi
        tl = (alo[0] + alo[1]) + (alo[2] + alo[3])
        th = (ahi[0] + ahi[1]) + (ahi[2] + ahi[3])
        tl = tl + pltpu.roll(tl, QROWS, 0)
        th = th + pltpu.roll(th, QROWS, 0)
        t = jnp.where(low, tl, th)
        trow = jnp.concatenate([t[s:s + 1, :] for s in range(SUBL)], axis=1)
        o_ref[pl.ds(n, 1), :] = x_ref[pl.ds(n, 1), :] + trow
        return carry

    lax.fori_loop(0, PEER_GT, token, 0)


def _peer_v(row3, c8, x2, tab):
    nt = row3.shape[0]
    sm = pl.BlockSpec((1, 1, PEER_GT * NSEL), lambda i: (i, 0, 0), memory_space=pltpu.SMEM)
    xs = pl.BlockSpec((PEER_GT, D_MODEL), lambda i: (i, 0))
    return pl.pallas_call(
        _peer_v_kernel,
        grid=(nt,),
        in_specs=[sm, pl.BlockSpec((PEER_GT, SUBL, NSEL), lambda i: (i, 0, 0)), xs, _vmem_full()],
        out_specs=xs,
        out_shape=jax.ShapeDtypeStruct(x2.shape, F32),
        compiler_params=_cparams(("parallel",)),
    )(row3, c8, x2, tab)


def _pack_table(t):
    e = t.shape[0]
    b = lax.bitcast_convert_type(t.astype(BF16), jnp.uint16).astype(jnp.uint32)
    w = b[:, :HALF] | (b[:, HALF:] << 16)
    return w.reshape(e * QROWS, LANES)


def _peer(hf, x2, pw):
    n = hf.shape[0]
    nt = n // PEER_TILE
    e_t, g_t = _peer_score(hf, pw['wq'], pw['kbd'])
    tok_major = lambda a: a.reshape(nt, NSEL, PEER_TILE).transpose(0, 2, 1).reshape(n, NSEL)
    e_tm = tok_major(e_t)
    par = jnp.bitwise_and(e_tm, 1)
    row3 = (jnp.right_shift(e_tm, 1) * SUBL).reshape(n // PEER_GT, 1, PEER_GT * NSEL)
    act = _peer_u(row3, par, hf, pw['u'])
    c = tok_major(g_t) * jax.nn.gelu(act, approximate=False)
    ce = jnp.where(par == 0, c, 0.0)[:, None, :]
    co = jnp.where(par == 1, c, 0.0)[:, None, :]
    c8 = jnp.concatenate([jnp.broadcast_to(ce, (n, QROWS, NSEL)), jnp.broadcast_to(co, (n, QROWS, NSEL))], axis=1)
    return _peer_v(row3, c8, x2, pw['v'])


def _masked_softmax(s, mask):
    s = jnp.where(mask, s, -jnp.inf)
    m = jnp.max(s, axis=-1, keepdims=True)
    m = jnp.where(jnp.isfinite(m), m, 0.0)
    e = jnp.where(mask, jnp.exp(s - m), 0.0)
    den = jnp.sum(e, axis=-1, keepdims=True)
    return e / jnp.where(den > 0, den, 1.0)


def _gather_pages(pool, page_table):
    g = pool[page_table]
    return g.reshape((g.shape[0], g.shape[1] * g.shape[2]) + g.shape[3:])


def _rms(x, g):
    return x * lax.rsqrt(jnp.mean(x * x, axis=-1, keepdims=True) + EPS) * g


def _compress(rows, w_c, pe):
    b, l, h, d = rows.shape
    blk = rows.reshape(b, l // CMP_BLOCK, CMP_BLOCK, h, d) + pe[:, None, :]
    return jnp.einsum('bjihd,ide->bjhe', blk, w_c)


DEC_PAGES = 8


def _online_update(m_ref, l_ref, a_ref, s, v, v_transposed=False):
    pv = _dot_nt if v_transposed else _dot
    m_old = m_ref[...]
    m_new = jnp.maximum(m_old, jnp.max(s, axis=-1, keepdims=True))
    m_safe = jnp.where(m_new > -jnp.inf, m_new, 0.0)
    p = jnp.exp(s - m_safe)
    alpha = jnp.exp(m_old - m_safe)
    l_ref[...] = alpha * l_ref[...] + jnp.sum(p, axis=-1, keepdims=True)
    a_ref[...] = alpha * a_ref[...] + pv(p.astype(BF16), v)
    m_ref[...] = m_new


def _decode_kernel(pt_ref, lam_ref, q_ref, qd_ref, ks_ref, vs_ref, kw_ref, vw_ref, kd_ref, vd_ref, oc_ref, ng_ref,
                   mem_ref, wk_ref, wv_ref, ex_ref, m64_ref, dgain_ref, *rest, ch, ts, past_len, nch):
    pages = rest[:4 * ch]
    ynsa_ref, ydf_ref = rest[4 * ch:4 * ch + 2]
    ms_ref, ls_ref, as_ref, md_ref, ld_ref, ad_ref = rest[4 * ch + 2:]
    c = pl.program_id(1)
    rs = NSA_H * ts
    rd = 2 * DF_H * ts

    @pl.when(c == 0)
    def _():
        ms_ref[...] = jnp.full(ms_ref.shape, -jnp.inf, F32)
        md_ref[...] = jnp.full(md_ref.shape, -jnp.inf, F32)
        ls_ref[...] = jnp.zeros(ls_ref.shape, F32)
        ld_ref[...] = jnp.zeros(ld_ref.shape, F32)
        as_ref[...] = jnp.zeros(as_ref.shape, F32)
        ad_ref[...] = jnp.zeros(ad_ref.shape, F32)

    lane = _iota((1, 256), 1)
    q = q_ref[...]
    qd = qd_ref[...]
    qbd = jnp.concatenate([jnp.where(jnp.right_shift(lane, 6) == h, q, 0.0) for h in range(NSA_H)],
                          axis=0).astype(BF16)
    qdbd = jnp.concatenate([jnp.where(jnp.right_shift(lane, 5) == 2 * h + cc, qd, 0.0)
                            for cc in range(2) for h in range(DF_H)], axis=0).astype(BF16)
    row_s = _iota((rs, 1), 0)
    row_d = _iota((rd, 1), 0)
    tok_s, tok_d = row_s, row_d
    sl_s = jnp.full((rs, 1), NSA_SLOPES[0], F32)
    sl_d = jnp.full((rd, 1), DF_SLOPES[0], F32)
    for g in range(1, rs // ts):
        tok_s = jnp.where(row_s >= g * ts, row_s - g * ts, tok_s)
        sl_s = jnp.where(row_s >= g * ts, NSA_SLOPES[g % NSA_H], sl_s)
    for g in range(1, rd // ts):
        tok_d = jnp.where(row_d >= g * ts, row_d - g * ts, tok_d)
        sl_d = jnp.where(row_d >= g * ts, DF_SLOPES[g % DF_H], sl_d)
    qpos_s = past_len + tok_s
    qpos_d = past_len + tok_d
    nk = ch * PAGE_SIZE
    kpos = c * nk + _iota((1, nk), 1)
    cat = lambda refs: jnp.concatenate([r[...] for r in refs], axis=1).astype(BF16)
    ksel, vsel = cat(pages[0:ch]), cat(pages[ch:2 * ch])
    kdf, vdf = cat(pages[2 * ch:3 * ch]), cat(pages[3 * ch:4 * ch])
    allowed = _dot(mem_ref[...].astype(BF16), ex_ref[...]) > 0.5
    s = jnp.where(allowed, _dot(qbd, ksel) - sl_s * (qpos_s - kpos).astype(F32), -jnp.inf)
    _online_update(ms_ref, ls_ref, as_ref, s, vsel, v_transposed=True)
    s = _dot(qdbd, kdf) - sl_d * (qpos_d - kpos).astype(F32)
    _online_update(md_ref, ld_ref, ad_ref, s, vdf, v_transposed=True)

    @pl.when(c == nch - 1)
    def _():
        pad = lambda ref: jnp.concatenate([ref[...], jnp.zeros((PAGE_SIZE - ts, 256), F32)], axis=0).astype(BF16)
        kposn = past_len + _iota((1, PAGE_SIZE), 1)
        dn_s = qpos_s - kposn
        dn_d = qpos_d - kposn
        ok = jnp.where(dn_s >= 0, jnp.where(jnp.right_shift(kposn, 6) == jnp.right_shift(qpos_s, 6), 1.0, 0.0),
                       0.0) > 0.5
        s_n = jnp.where(ok, _dot_nt(qbd, pad(ks_ref)) - sl_s * dn_s.astype(F32), -jnp.inf)
        _online_update(ms_ref, ls_ref, as_ref, s_n, pad(vs_ref))
        o_s = as_ref[...] / ls_ref[...]
        s_n = jnp.where(dn_d >= 0, _dot_nt(qdbd, pad(kd_ref)) - sl_d * dn_d.astype(F32), -jnp.inf)
        _online_update(md_ref, ld_ref, ad_ref, s_n, pad(vd_ref))
        od = ad_ref[...] / ld_ref[...]
        o_d = od[:rs, :] - lam_ref[0] * od[rs:, :]
        wb = wk_ref.shape[1]
        dw = qpos_s - (past_len - wb + _iota((1, wb), 1))
        okw = jnp.where(dw >= 0, jnp.where(dw < WINDOW, 1.0, 0.0), 0.0) > 0.5
        s1 = jnp.where(okw, _dot(qbd, wk_ref[...].astype(BF16)) - sl_s * dw.astype(F32), -jnp.inf)
        okn = jnp.where(dn_s >= 0, jnp.where(dn_s < WINDOW, 1.0, 0.0), 0.0) > 0.5
        s2 = jnp.where(okn, _dot_nt(qbd, pad(kw_ref)) - sl_s * dn_s.astype(F32), -jnp.inf)
        m = jnp.maximum(jnp.max(s1, axis=-1, keepdims=True), jnp.max(s2, axis=-1, keepdims=True))
        e1 = jnp.exp(s1 - m)
        e2 = jnp.exp(s2 - m)
        den = jnp.sum(e1, axis=-1, keepdims=True) + jnp.sum(e2, axis=-1, keepdims=True)
        o_w = (_dot_nt(e1.astype(BF16), wv_ref[...].astype(BF16)) + _dot(e2.astype(BF16), pad(vw_ref))) / den
        sig = jax.nn.sigmoid(ng_ref[...])
        y = jnp.zeros((ts, 256), F32)
        od_t = jnp.zeros((ts, 256), F32)
        for h in range(NSA_H):
            hm = jnp.right_shift(lane, 6) == h
            rows = slice(h * ts, (h + 1) * ts)
            yh = (sig[:, h:h + 1] * oc_ref[...] + sig[:, NSA_H + h:NSA_H + h + 1] * o_s[rows, :]
                  + sig[:, 2 * NSA_H + h:2 * NSA_H + h + 1] * o_w[rows, :])
            y = y + jnp.where(hm, yh, 0.0)
            od_t = od_t + jnp.where(hm, o_d[rows, :], 0.0)
        ynsa_ref[...] = y
        ss = _seg_sum(od_t * od_t, m64_ref[...])
        ydf_ref[...] = od_t * lax.rsqrt(ss * (1.0 / DF_DV) + EPS) * dgain_ref[...]


def _sample_decode(zs, o_c, member, c, page_table, dec, bsz, ts, past_len):
    n_pages = page_table.shape[1]
    ch = math.gcd(DEC_PAGES, n_pages)
    nch = n_pages // ch
    sbc = ch * (PAGE_SIZE // SEL_BLOCK)
    rs = NSA_H * ts
    layer = dec['l']
    mem4 = member.reshape(bsz, rs, nch, sbc).transpose(0, 2, 1, 3)
    expand = jnp.asarray(np.arange(sbc)[:, None] == (np.arange(ch * PAGE_SIZE)[None, :] // SEL_BLOCK), BF16)
    pool = lambda a: a.transpose(0, 1, 3, 4, 2).reshape(a.shape[0], a.shape[1], 256, a.shape[2])
    wbuf = pool
    zcol = lambda off, w=256: pl.BlockSpec((ts, w), lambda b, cc, pt: (b, off // w))
    par = lambda s: pl.BlockSpec(s, lambda b, cc, pt: (0,) * len(s))
    wspec = pl.BlockSpec((None, None, 256, dec['win_k'].shape[2]), lambda b, cc, pt: (layer, b, 0, 0))
    page = lambda k: pl.BlockSpec((None, None, 256, PAGE_SIZE),
                                  lambda b, cc, pt: (layer, pt[b * n_pages + cc * ch + k], 0, 0))
    in_specs = [pl.BlockSpec(memory_space=pltpu.SMEM),
                zcol(OFF_Q), zcol(OFF_QD), zcol(OFF_KS), zcol(OFF_VS), zcol(OFF_KW), zcol(OFF_VW), zcol(OFF_KD),
                zcol(OFF_VD), pl.BlockSpec((ts, 256), lambda b, cc, pt: (b, 0)), zcol(OFF_NG, 128),
                pl.BlockSpec((None, None, rs, sbc), lambda b, cc, pt: (b, cc, 0, 0)),
                wspec, wspec, par((sbc, ch * PAGE_SIZE)), par((256, 256)), par((1, 256))]
    in_specs += [page(k) for _ in range(4) for k in range(ch)]
    ospec = pl.BlockSpec((ts, 256), lambda b, cc, pt: (b, 0))
    pools = [pool(dec[nm]) for nm in ('sel_k', 'sel_v', 'diff_k', 'diff_v')]
    page_args = [p for p in pools for _ in range(ch)]
    sd = jax.ShapeDtypeStruct((bsz * ts, 256), F32)
    grid_spec = pltpu.PrefetchScalarGridSpec(
        num_scalar_prefetch=1, grid=(bsz, nch), in_specs=in_specs, out_specs=[ospec, ospec],
        scratch_shapes=[pltpu.VMEM((rs, 1), F32), pltpu.VMEM((rs, 1), F32), pltpu.VMEM((rs, 256), F32),
                        pltpu.VMEM((2 * rs, 1), F32), pltpu.VMEM((2 * rs, 1), F32), pltpu.VMEM((2 * rs, 256), F32)])
    return pl.pallas_call(
        functools.partial(_decode_kernel, ch=ch, ts=ts, past_len=past_len, nch=nch),
        grid_spec=grid_spec,
        out_shape=[sd, sd],
        compiler_params=_cparams(("parallel", "arbitrary")),
    )(page_table.reshape(-1), dec['lam'], zs, zs, zs, zs, zs, zs, zs, zs, o_c, zs, mem4,
      wbuf(dec['win_k']), wbuf(dec['win_v']), expand, dec['m64'], dec['df_gain'], *page_args)


def _sample_attn(zs, c, page_table, lw, dec, bsz, t_len, past_len):
    col = lambda off, w=256: zs[:, off:off + w].reshape(bsz, t_len, -1)
    hd = lambda a: a.reshape(bsz, t_len, NSA_H, HD)
    q = hd(col(OFF_Q))
    kc_r, vc_r = hd(col(OFF_KC)), hd(col(OFF_VC))
    qpos = past_len + jnp.arange(t_len)
    ns = jnp.asarray(NSA_SLOPES, F32)
    n_new = (past_len + t_len) // CMP_BLOCK - past_len // CMP_BLOCK

    def cmp_rows(pool, new, w_c, pe):
        zc = _compress(_gather_pages(pool, page_table), w_c, pe)
        if n_new > 0:
            zc = jnp.concatenate([zc, _compress(new[:, :n_new * CMP_BLOCK], w_c, pe)], axis=1)
        return zc

    kc = _rms(cmp_rows(c['cmp_k'], kc_r, lw['nsa_ck_w'], lw['nsa_ck_pe']), lw['nsa_kn'][0])
    vc = cmp_rows(c['cmp_v'], vc_r, lw['nsa_cv_w'], lw['nsa_cv_pe'])
    nc = kc.shape[1]
    end = jnp.arange(nc) * CMP_BLOCK + (CMP_BLOCK - 1)
    dist = (qpos[:, None] - end[None, :]).astype(F32)
    s = jnp.einsum('bthd,bjhd->bhtj', q, kc).astype(F32) - ns[:, None, None] * dist
    p_c = _masked_softmax(s, dist >= 0)
    o_c = jnp.einsum('bhtj,bjhd->bthd', p_c, vc)
    nsb = past_len // SEL_BLOCK
    kp = min(SEL_TOPK - 1, nsb)
    imp = p_c[..., :nsb * 2].reshape(bsz, NSA_H, t_len, nsb, 2).sum(-1)
    cur = qpos // SEL_BLOCK
    imp = jnp.where(jnp.arange(nsb)[None, :] < cur[:, None], imp, -1.0)
    _, idx = lax.top_k(imp, kp)
    hit = (idx[..., None] == jnp.arange(nsb)) & (idx < cur[:, None])[..., None]
    member = jnp.any(hit, axis=-2).astype(F32).reshape(bsz, NSA_H * t_len, nsb)
    return _sample_decode(zs, o_c.reshape(bsz * t_len, MIX_W), member, c, page_table, dec, bsz, t_len, past_len)


def _block_ones(gs):
    i = np.arange(256)
    return jnp.asarray((i[:, None] // gs) == (i[None, :] // gs), BF16)


def _tile_heads(v, reps):
    return jnp.tile(v.astype(F32), reps).reshape(1, -1)


def _prep_layer(lw, l):
    m64, m32 = _block_ones(64), _block_ones(32)
    w_in = lw['w_in']
    src = np.cumsum([0, 256, 256, 256, 256, 256, 256, 256, 12, RW_P, 256, 256, 256, 256, 256, 256, 4096])
    names = ['q', 'kc', 'vc', 'ks', 'vs', 'kw', 'vw', 'ng', 'rw', 'bg', 'cg', 'xc', 'qd', 'kd', 'vd', 'gate']
    dst = dict(q=OFF_Q, kc=OFF_KC, vc=OFF_VC, ks=OFF_KS, vs=OFF_VS, kw=OFF_KW, vw=OFF_VW, ng=OFF_NG, rw=OFF_RW,
               bg=OFF_BG, cg=OFF_CG, xc=OFF_XC, qd=OFF_QD, kd=OFF_KD, vd=OFF_VD, gate=OFF_GATE)
    order = sorted(names, key=lambda nm: dst[nm])
    pieces = []
    pos = 0
    for nm in order:
        k = names.index(nm)
        assert dst[nm] == pos, (nm, dst[nm], pos)
        piece = w_in[:, int(src[k]):int(src[k + 1])]
        if nm == 'ng':
            piece = jnp.pad(piece, ((0, 0), (0, 128 - 3 * NSA_H)))
        pieces.append(piece)
        pos += piece.shape[1]
    assert pos == P_PAD
    w_pad = jnp.concatenate(pieces, axis=1).astype(BF16)
    gains = jnp.concatenate([
        _tile_heads(lw['nsa_qn'], 4) * (HD ** -0.5), _tile_heads(lw['nsa_kn'][1], 4), _tile_heads(lw['nsa_kn'][2], 4),
        _tile_heads(lw['df_qn'], 8) * (DF_DQK ** -0.5), _tile_heads(lw['df_kn'], 8),
        jnp.zeros((3, 256), F32)], axis=0)
    eye4 = jnp.eye(NSA_H, dtype=F32)
    bd = lambda w: jnp.einsum('hg,ide->ihdge', eye4, w).reshape(CMP_K, 256).astype(BF16)
    lam_init = 0.8 - 0.6 * math.exp(-0.3 * l)
    lam = (jnp.exp(jnp.sum(lw['df_lq1'] * lw['df_lk1'])) - jnp.exp(jnp.sum(lw['df_lq2'] * lw['df_lk2'])) + lam_init)
    lora = lambda w, a, b: jnp.zeros((128, 256), F32).at[a:b].set(w).astype(BF16)
    seqw = dict(mu=lw['rw_mu'].reshape(1, RW_P), w0=lw['rw_w0'].reshape(1, 256), a0=lw['rw_a0'].reshape(1, 256),
                lw=lora(lw['rw_w2'], 0, 32), la=lora(lw['rw_a2'], 32, 64), lg=lora(lw['rw_g2'], 64, 128),
                kk=lw['rw_kk'].reshape(1, 256), ka=lw['rw_ka'].reshape(1, 256),
                cw=jnp.pad(lw['conv_w'], ((0, 8 - CONV_K), (0, 0))), cb=lw['conv_b'].reshape(1, 256), m64=m64)
    mw = dict(wbr=lw['w_br'].astype(BF16), wo=lw['w_o'].astype(BF16), lnw=lw['rw_lnw'].reshape(1, 256),
              lnb=lw['rw_lnb'].reshape(1, 256), rk=lw['rw_rk'].reshape(1, 256), m64=m64,
              nx=lw['norm_x'].reshape(1, D_MODEL), wxq=lw['w_xq'].astype(BF16),
              qg=_tile_heads(lw['x_qn'], 4) * (HD ** -0.5))
    keys = lw['pk_keys'].reshape(PK_H * 2, N_KEYS, PK_DK // 2)
    eye16 = jnp.eye(PK_H * 2, dtype=F32)
    kbd = jnp.einsum('gf,gkd->gkfd', eye16, keys).reshape(PK_H * 2 * N_KEYS, PK_H * PK_DK).astype(BF16)
    pw = dict(wq=lw['pk_wq'].astype(BF16), kbd=kbd, u=_pack_table(lw['pk_u']), v=_pack_table(lw['pk_v']))
    return dict(
        g_mix=lw['norm_mix'].reshape(1, D_MODEL), w_pad=w_pad, gains=gains, m64=m64, m32=m32,
        wk_bd=bd(lw['nsa_ck_w']), wv_bd=bd(lw['nsa_cv_w']),
        pek=jnp.tile(lw['nsa_ck_pe'], (1, NSA_H)).reshape(1, CMP_K),
        pev=jnp.tile(lw['nsa_cv_pe'], (1, NSA_H)).reshape(1, CMP_K),
        gk0=_tile_heads(lw['nsa_kn'][0], 4), lam=lam.reshape(1).astype(F32), lam_init=lam_init,
        df_gain=_tile_heads(lw['df_subln'], 4) * (1.0 - lam_init), seqw=seqw, mw=mw, pw=pw,
        g_mem=lw['norm_mem'].reshape(1, D_MODEL),
        wkv=jnp.concatenate([lw['w_xk'], lw['w_xv']], axis=1).astype(BF16),
        kg=_tile_heads(lw['x_kn'], 4), wxo=lw['w_xo'].astype(BF16), nf=lw['norm_ffn'].reshape(1, D_MODEL))


def _pad_rows(x, mult):
    n = x.shape[0]
    p = (-n) % mult
    return x if p == 0 else jnp.pad(x, ((0, p),) + ((0, 0),) * (x.ndim - 1))


def _tail(x2d, z, ys, rw, mk, mv, pl_, bsz, t_len):
    y_nsa, y_scan, y_cv, y_df = ys
    r, k, v, g = rw
    x1, qx = _merge(x2d, z, y_nsa, y_scan, r, k, v, g, y_cv, y_df, pl_['mw'])
    x2, hf = _xattn(qx, mk, mv, x1, pl_['wxo'], pl_['nf'], bsz, t_len)
    n = x2.shape[0]
    out = _peer(_pad_rows(hf, PEER_TILE), _pad_rows(x2, PEER_TILE), pl_['pw'])
    return out[:n]


def _prompt_layer(x2d, mem2d, pl_, bsz, t_len):
    n = bsz * t_len
    z, zb = _proj(x2d, pl_['g_mix'], pl_['w_pad'], pl_['gains'], pl_['m64'], pl_['m32'])
    kc_raw, vc_raw = z[:, OFF_KC:OFF_KC + 256], z[:, OFF_VC:OFF_VC + 256]
    cmp4 = _cmp(kc_raw.reshape(n // SEL_BLOCK, 2 * CMP_K), vc_raw.reshape(n // SEL_BLOCK, 2 * CMP_K),
                pl_['wk_bd'], pl_['wv_bd'], pl_['pek'], pl_['pev'], pl_['m64'], pl_['gk0'])
    nsb = t_len // SEL_BLOCK
    expand = jnp.asarray(np.arange(nsb)[:, None] == (np.arange(t_len)[None, :] // SEL_BLOCK), BF16)
    y_nsa = _nsa_prompt(z, zb, cmp4, expand, bsz, t_len)
    y_df = _diff_prompt(pl_['lam'], z, zb, pl_['m64'], pl_['df_gain'], bsz, t_len)
    seq = _seq(z, jnp.zeros((bsz, 8, RW_P), F32), jnp.zeros((bsz, 8, 256), F32), pl_['seqw'], bsz, t_len)
    r, w, k, v, am, bb, g, y_cv, ulast = seq
    y_scan, s_fin = _rwkv_scan((r, w, k, v, am, bb), jnp.zeros((bsz, RW_H, RW_N, RW_N), F32), bsz, t_len)
    mk, mv = _memkv(mem2d, pl_['g_mem'], pl_['wkv'], pl_['m64'], pl_['kg'])
    mlen = mem2d.shape[0] // bsz
    mk3, mv3 = mk.reshape(bsz, mlen, 256), mv.reshape(bsz, mlen, 256)
    x_out = _tail(x2d, z, (y_nsa, y_scan, y_cv, y_df), (r, k, v, g), mk3.astype(BF16), mv3.astype(BF16),
                  pl_, bsz, t_len)
    st = lambda off: z[:, off:off + 256].reshape(bsz, t_len, NSA_H, HD)
    wb = min(WINDOW, t_len)
    shift = z[:, OFF_RW:OFF_RW + RW_P].reshape(bsz, t_len, RW_P)[:, -1]
    states = (st(OFF_KC), st(OFF_VC), st(OFF_KS), st(OFF_VS), st(OFF_KW)[:, t_len - wb:], st(OFF_VW)[:, t_len - wb:],
              st(OFF_KD), st(OFF_VD), mk3.reshape(bsz, mlen, MX_H, HD), mv3.reshape(bsz, mlen, MX_H, HD),
              s_fin, shift, ulast[:, 8 - (CONV_K - 1):])
    return x_out, states


def _sample_layer(x2d, c, page_table, lw, pl_, bsz, t_len, past_len):
    z, _ = _proj(x2d, pl_['g_mix'], pl_['w_pad'], pl_['gains'], pl_['m64'], pl_['m32'])
    dec = dict(c['full'], lam=pl_['lam'], m64=pl_['m64'], df_gain=pl_['df_gain'])
    y_nsa, y_df = _sample_attn(z, c, page_table, lw, dec, bsz, t_len, past_len)
    shift0 = jnp.zeros((bsz, 8, RW_P), F32).at[:, 7].set(c['rwkv_shift'])
    conv0 = jnp.zeros((bsz, 8, 256), F32).at[:, 8 - (CONV_K - 1):].set(c['conv'])
    seq = _seq(z, shift0, conv0, pl_['seqw'], bsz, t_len)
    r, w, k, v, am, bb, g, y_cv, ulast = seq
    y_scan, s_fin = _rwkv_scan((r, w, k, v, am, bb), c['rwkv'], bsz, t_len)
    mk = c['mem_k'].reshape(bsz, -1, 256).astype(BF16)
    mv = c['mem_v'].reshape(bsz, -1, 256).astype(BF16)
    x_out = _tail(x2d, z, (y_nsa, y_scan, y_cv, y_df), (r, k, v, g), mk, mv, pl_, bsz, t_len)
    st = lambda off: z[:, off:off + 256].reshape(bsz, t_len, NSA_H, HD)
    shift = z[:, OFF_RW:OFF_RW + RW_P].reshape(bsz, t_len, RW_P)[:, -1]
    states = (st(OFF_KC), st(OFF_VC), st(OFF_KS), st(OFF_VS), st(OFF_KW), st(OFF_VW), st(OFF_KD), st(OFF_VD),
              s_fin, shift, ulast[:, 8 - (CONV_K - 1):])
    return x_out, states


def kernel(x_prompt, x_sample, mem_prompt, cache_cmp_k, cache_cmp_v, cache_sel_k, cache_sel_v, cache_win_k, cache_win_v, cache_diff_k, cache_diff_v, cache_mem_k, cache_mem_v, state_rwkv, state_rwkv_shift, state_conv, page_table, norm_mix, w_in, nsa_qn, nsa_kn, nsa_ck_w, nsa_ck_pe, nsa_cv_w, nsa_cv_pe, rw_mu, rw_w0, rw_w2, rw_a0, rw_a2, rw_g2, rw_kk, rw_ka, rw_rk, rw_lnw, rw_lnb, conv_w, conv_b, df_qn, df_kn, df_lq1, df_lk1, df_lq2, df_lk2, df_subln, w_br, w_o, norm_x, norm_mem, w_xq, w_xk, w_xv, x_qn, x_kn, w_xo, norm_ffn, pk_wq, pk_keys, pk_u, pk_v):
    bp, tp, _ = x_prompt.shape
    bs, ts, _ = x_sample.shape
    depth = w_in.shape[0]
    past_len = page_table.shape[1] * PAGE_SIZE
    weights = dict(norm_mix=norm_mix, w_in=w_in, nsa_qn=nsa_qn, nsa_kn=nsa_kn, nsa_ck_w=nsa_ck_w,
                   nsa_ck_pe=nsa_ck_pe, nsa_cv_w=nsa_cv_w, nsa_cv_pe=nsa_cv_pe, rw_mu=rw_mu, rw_w0=rw_w0,
                   rw_w2=rw_w2, rw_a0=rw_a0, rw_a2=rw_a2, rw_g2=rw_g2, rw_kk=rw_kk, rw_ka=rw_ka, rw_rk=rw_rk,
                   rw_lnw=rw_lnw, rw_lnb=rw_lnb, conv_w=conv_w, conv_b=conv_b, df_qn=df_qn, df_kn=df_kn,
                   df_lq1=df_lq1, df_lk1=df_lk1, df_lq2=df_lq2, df_lk2=df_lk2, df_subln=df_subln, w_br=w_br,
                   w_o=w_o, norm_x=norm_x, norm_mem=norm_mem, w_xq=w_xq, w_xk=w_xk, w_xv=w_xv, x_qn=x_qn,
                   x_kn=x_kn, w_xo=w_xo, norm_ffn=norm_ffn, pk_wq=pk_wq, pk_keys=pk_keys, pk_u=pk_u, pk_v=pk_v)
    xp = x_prompt.reshape(bp * tp, D_MODEL)
    xs = x_sample.reshape(bs * ts, D_MODEL)
    mem2d = mem_prompt.reshape(-1, D_MODEL)
    p_acc, s_acc = [], []
    for l in range(depth):
        lw = {k_: v_[l] for k_, v_ in weights.items()}
        pl_ = _prep_layer(lw, l)
        full = dict(l=l, sel_k=cache_sel_k, sel_v=cache_sel_v, diff_k=cache_diff_k, diff_v=cache_diff_v,
                    win_k=cache_win_k, win_v=cache_win_v)
        c = dict(cmp_k=cache_cmp_k[l], cmp_v=cache_cmp_v[l], mem_k=cache_mem_k[l], mem_v=cache_mem_v[l],
                 rwkv=state_rwkv[l], rwkv_shift=state_rwkv_shift[l], conv=state_conv[l], full=full)
        xp, ps = _prompt_layer(xp, mem2d, pl_, bp, tp)
        xs, ss = _sample_layer(xs, c, page_table, lw, pl_, bs, ts, past_len)
        p_acc.append(ps)
        s_acc.append(ss)
    p_out = [jnp.stack(a) for a in zip(*p_acc)]
    s_out = [jnp.stack(a) for a in zip(*s_acc)]
    return (xp.reshape(bp, tp, D_MODEL), xs.reshape(bs, ts, D_MODEL), *p_out, *s_out)
```
